```python
import jax, jax.numpy as jnp
from jax import lax
import numpy as np

D_MODEL = 1024
BATCH = 16
SEQ = 2048
DEPTH = 1

CHUNK = 64
Q_BLOCK = 128
HEAD_DIM = 64
D_MIX = D_MODEL
SB_HEADS = 8
SB_WIDTH = SB_HEADS * HEAD_DIM
RW_WIDTH = D_MIX - SB_WIDTH
RW_HEADS = RW_WIDTH // HEAD_DIM
DECAY_LORA = 64
AAA_LORA = 64
GATE_LORA = 128
RW_PROJ = 3 * RW_WIDTH + DECAY_LORA + AAA_LORA + GATE_LORA
PROJ_WIDTH = 3 * SB_WIDTH + RW_PROJ
D_FF = 2816
FFN_HALF = 0.5
RMS_EPS = 1e-6
LNX_EPS = 64e-5

kernel_name = "hybrid_stickbreak_rwkv7_macaron"


def rms_norm(x, g, eps=RMS_EPS):
    xf = x.astype(jnp.float32)
    ms = jnp.mean(xf * xf, axis=-1, keepdims=True)
    return (xf * lax.rsqrt(ms + eps) * g.astype(jnp.float32)).astype(x.dtype)


def swiglu(x, w_gate, w_up, w_down):
    return (jax.nn.silu(x @ w_gate) * (x @ w_up)) @ w_down


def token_shift(u):
    return jnp.pad(u[:, :-1], ((0, 0), (1, 0), (0, 0)))


def stick_breaking_attention(q, k, v):
    S, Dh = q.shape[1], q.shape[-1]
    scale = Dh ** -0.5
    outs = []
    for blk in range(S // Q_BLOCK):
        q0 = blk * Q_BLOCK
        kv_len = q0 + Q_BLOCK
        qb, kb, vb = q[:, q0:kv_len], k[:, :kv_len], v[:, :kv_len]
        z = jnp.einsum('bqhd,bkhd->bhqk', qb, kb).astype(jnp.float32) * scale
        q_pos = q0 + jnp.arange(Q_BLOCK)[:, None]
        k_pos = jnp.arange(kv_len)[None, :]
        mask = k_pos < q_pos
        log_beta = jax.nn.log_sigmoid(z)
        log_1m_beta = jnp.where(mask, jax.nn.log_sigmoid(-z), 0.0)
        log_remain = lax.cumsum(log_1m_beta, axis=3, reverse=True) - log_1m_beta
        att = jnp.where(mask, jnp.exp(log_beta + log_remain), 0.0)
        outs.append(jnp.einsum('bhqk,bkhd->bqhd', att.astype(vb.dtype), vb))
    return jnp.concatenate(outs, axis=1)


def rwkv7_recurrence(r, decay, k, v, a_vec, b_vec):
    B, S, H, N = r.shape
    n_chunks = S // CHUNK

    def to_chunks(t):
        return t.astype(jnp.float32).transpose(1, 0, 2, 3).reshape(n_chunks, CHUNK, B, H, N)

    xs = (to_chunks(r), to_chunks(decay), to_chunks(k), to_chunks(v),
          to_chunks(a_vec), to_chunks(b_vec))

    def step(state, inp):
        r_t, w_t, k_t, v_t, a_t, b_t = inp
        sa = jnp.einsum('bhvk,bhk->bhv', state, a_t)
        state = (state * w_t[:, :, None, :] + sa[..., None] * b_t[:, :, None, :]
                 + v_t[..., None] * k_t[:, :, None, :])
        return state, jnp.einsum('bhvk,bhk->bhv', state, r_t)

    def chunk_step(state, chunk_xs):
        return lax.scan(step, state, chunk_xs)

    state0 = jnp.zeros((B, H, N, N), jnp.float32)
    _, y = lax.scan(chunk_step, state0, xs)
    return y.reshape(S, B, H, N).transpose(1, 0, 2, 3)


def rwkv7_time_mix(u, mu, w0, w2, a0, a2, g2, k_k, k_a, r_k, ln_w, ln_b):
    B, S, _ = u.shape
    u = u + (token_shift(u) - u) * mu
    x_r, x_k, x_v, x_w, x_a, x_g = jnp.split(
        u, [RW_WIDTH, 2 * RW_WIDTH, 3 * RW_WIDTH, 3 * RW_WIDTH + DECAY_LORA,
            3 * RW_WIDTH + DECAY_LORA + AAA_LORA], axis=-1)
    log_w = -jax.nn.softplus(-(w0 + jnp.tanh(x_w) @ w2)) - 0.5
    decay = jnp.exp(-jnp.exp(log_w.astype(jnp.float32)))
    a = jax.nn.sigmoid(a0 + x_a @ a2).astype(jnp.float32)
    g = jax.nn.sigmoid(x_g) @ g2

    def heads(t):
        return t.reshape(B, S, RW_HEADS, HEAD_DIM)

    r = heads(x_r).astype(jnp.float32)
    v = heads(x_v).astype(jnp.float32)
    kk = heads(x_k * k_k).astype(jnp.float32)
    kk = kk / jnp.maximum(jnp.linalg.norm(kk, axis=-1, keepdims=True), 1e-12)
    kf = heads(x_k * (1.0 + (a - 1.0) * k_a)).astype(jnp.float32)
    a = heads(a)
    y = rwkv7_recurrence(r, heads(decay), kf, v, -kk, kk * a)
    mean = jnp.mean(y, axis=-1, keepdims=True)
    var = jnp.mean(jnp.square(y - mean), axis=-1, keepdims=True)
    yn = (y - mean) * lax.rsqrt(var + LNX_EPS)
    yn = yn * ln_w.astype(jnp.float32).reshape(RW_HEADS, HEAD_DIM) + ln_b.astype(jnp.float32).reshape(RW_HEADS, HEAD_DIM)
    bonus = jnp.sum(r * kf * r_k.astype(jnp.float32), axis=-1, keepdims=True) * v
    return ((yn + bonus).reshape(B, S, RW_WIDTH) * g.astype(jnp.float32)).astype(u.dtype)


def setup_inputs(seed: int = 0) -> dict:
    key = jax.random.key(seed)
    ks = jax.random.split(key, 26)
    L, D = DEPTH, D_MODEL
    f32 = jnp.float32

    def nrm(k, shape, scale):
        return scale * jax.random.normal(k, shape, f32)

    def gain(k, shape):
        return 1.0 + 0.05 * jax.random.normal(k, shape, f32)

    return {
        "x": jax.random.normal(ks[0], (BATCH, SEQ, D), f32),
        "norm_ffn1": gain(ks[1], (L, D)),
        "ffn1_gate": nrm(ks[2], (L, D, D_FF), D ** -0.5),
        "ffn1_up": nrm(ks[3], (L, D, D_FF), D ** -0.5),
        "ffn1_down": nrm(ks[4], (L, D_FF, D), D_FF ** -0.5),
        "norm_mix": gain(ks[5], (L, D)),
        "w_in": nrm(ks[6], (L, D, PROJ_WIDTH), D ** -0.5),
        "sb_q_norm": gain(ks[7], (L, HEAD_DIM)),
        "sb_k_norm": gain(ks[8], (L, HEAD_DIM)),
        "sb_out_norm": gain(ks[9], (L, SB_WIDTH)),
        "rw_mu": jax.random.uniform(ks[10], (L, RW_PROJ), f32),
        "rw_w0": jax.random.uniform(ks[11], (L, RW_WIDTH), f32, minval=-5.0, maxval=-1.0),
        "rw_w2": nrm(ks[12], (L, DECAY_LORA, RW_WIDTH), 0.5 * DECAY_LORA ** -0.5),
        "rw_a0": nrm(ks[13], (L, RW_WIDTH), 0.1),
        "rw_a2": nrm(ks[14], (L, AAA_LORA, RW_WIDTH), 0.5 * AAA_LORA ** -0.5),
        "rw_g2": nrm(ks[15], (L, GATE_LORA, RW_WIDTH), GATE_LORA ** -0.5),
        "rw_k_k": 0.85 + 0.05 * jax.random.normal(ks[16], (L, RW_WIDTH), f32),
        "rw_k_a": gain(ks[17], (L, RW_WIDTH)),
        "rw_r_k": nrm(ks[18], (L, RW_HEADS, HEAD_DIM), 0.1),
        "rw_ln_w": gain(ks[19], (L, RW_WIDTH)),
        "rw_ln_b": nrm(ks[20], (L, RW_WIDTH), 0.01),
        "w_out": nrm(ks[21], (L, D_MIX, D), D_MIX ** -0.5),
        "norm_ffn2": gain(ks[22], (L, D)),
        "ffn2_gate": nrm(ks[23], (L, D, D_FF), D ** -0.5),
        "ffn2_up": nrm(ks[24], (L, D, D_FF), D ** -0.5),
        "ffn2_down": nrm(ks[25], (L, D_FF, D), D_FF ** -0.5),
    }


def reference(x, norm_ffn1, ffn1_gate, ffn1_up, ffn1_down, norm_mix, w_in,
              sb_q_norm, sb_k_norm, sb_out_norm, rw_mu, rw_w0, rw_w2, rw_a0, rw_a2,
              rw_g2, rw_k_k, rw_k_a, rw_r_k, rw_ln_w, rw_ln_b, w_out, norm_ffn2,
              ffn2_gate, ffn2_up, ffn2_down):
    B, S, _ = x.shape
    for l in range(DEPTH):
        x = x + FFN_HALF * swiglu(rms_norm(x, norm_ffn1[l]), ffn1_gate[l], ffn1_up[l], ffn1_down[l])

        h = rms_norm(x, norm_mix[l])
        p = h @ w_in[l]
        sb_q, sb_k, sb_v, rw_in = jnp.split(p, [SB_WIDTH, 2 * SB_WIDTH, 3 * SB_WIDTH], axis=-1)

        q = rms_norm(sb_q.reshape(B, S, SB_HEADS, HEAD_DIM), sb_q_norm[l])
        k = rms_norm(sb_k.reshape(B, S, SB_HEADS, HEAD_DIM), sb_k_norm[l])
        v = sb_v.reshape(B, S, SB_HEADS, HEAD_DIM)
        o_sb = stick_breaking_attention(q, k, v)
        o_sb = rms_norm(o_sb, sb_out_norm[l].reshape(SB_HEADS, HEAD_DIM)).reshape(B, S, SB_WIDTH)

        o_rw = rwkv7_time_mix(rw_in, rw_mu[l], rw_w0[l], rw_w2[l], rw_a0[l], rw_a2[l], rw_g2[l],
                              rw_k_k[l], rw_k_a[l], rw_r_k[l], rw_ln_w[l], rw_ln_b[l])

        x = x + jnp.concatenate([o_sb, o_rw], axis=-1) @ w_out[l]

        x = x + FFN_HALF * swiglu(rms_norm(x, norm_ffn2[l]), ffn2_gate[l], ffn2_up[l], ffn2_down[l])
    return x
```

```python
import functools

import jax
import jax.numpy as jnp
from jax import lax
from jax.experimental import pallas as pl
from jax.experimental.pallas import tpu as pltpu

F32 = jnp.float32
BF16 = jnp.bfloat16

HEAD_DIM = 64
SB_WIDTH = 512
RW_WIDTH = 512
DECAY_LORA = 64
AAA_LORA = 64
GATE_LORA = 128
RW_PROJ = 3 * RW_WIDTH + DECAY_LORA + AAA_LORA + GATE_LORA
RMS_EPS = 1e-6
LNX_EPS = 64e-5
KK_NORM_FLOOR = 1e-12

LANES = 128
MXU_DIM = 256
RW_CHUNK = 64
SB_BLOCK = 128
VMEM_LIMIT = 56 * 1024 * 1024


def _const_spec(shape):
    zeros = (0,) * len(shape)
    return pl.BlockSpec(shape, lambda *_: zeros, pipeline_mode=pl.Buffered(1))


def _dot(a, b):
    return jnp.dot(a.astype(BF16), b.astype(BF16), preferred_element_type=F32)


def _dot_nt(a, b):
    return lax.dot_general(a.astype(BF16), b.astype(BF16), (((1,), (1,)), ((), ())),
                           preferred_element_type=F32)


def _dot_f32(a, b):
    return jnp.dot(a, b, precision=lax.Precision.HIGHEST, preferred_element_type=F32)


def _rms_norm_rows(x, gain):
    ms = jnp.mean(x * x, axis=-1, keepdims=True)
    return x * lax.rsqrt(ms + RMS_EPS) * gain


def _head_reduce(t, bd, two_pass):
    outs = []
    for half in range(t.shape[1] // MXU_DIM):
        th = t[:, half * MXU_DIM:(half + 1) * MXU_DIM]
        hi = th.astype(BF16)
        red = jnp.dot(hi, bd, preferred_element_type=F32)
        if two_pass:
            lo = (th - hi.astype(F32)).astype(BF16)
            red = red + jnp.dot(lo, bd, preferred_element_type=F32)
        outs.append(red)
    return jnp.concatenate(outs, axis=1)


def _swiglu_residual(x, gain_ref, wg_ref, wu_ref, wd_ref, ff_tile):
    h = _rms_norm_rows(x, gain_ref[...]).astype(BF16)
    acc = x
    for c in range(wg_ref.shape[1] // ff_tile):
        cols = slice(c * ff_tile, (c + 1) * ff_tile)
        gate = jnp.dot(h, wg_ref[:, cols], preferred_element_type=F32)
        up = jnp.dot(h, wu_ref[:, cols], preferred_element_type=F32)
        act = (gate * jax.nn.sigmoid(gate) * up * 0.5).astype(BF16)
        acc = acc + jnp.dot(act, wd_ref[cols, :], preferred_element_type=F32)
    return acc


def _ffn_kernel(x_ref, gain_ref, wg_ref, wu_ref, wd_ref, o_ref, *, ff_tile):
    o_ref[...] = _swiglu_residual(x_ref[...], gain_ref, wg_ref, wu_ref, wd_ref, ff_tile)


def _proj_ffn_kernel(x_ref, osb_ref, orw_ref, wo_ref, gain_ref, wg_ref, wu_ref, wd_ref,
                     o_ref, *, ff_tile):
    mixed = (jnp.dot(osb_ref[...], wo_ref[:SB_WIDTH, :], preferred_element_type=F32)
             + jnp.dot(orw_ref[...], wo_ref[SB_WIDTH:, :], preferred_element_type=F32))
    x = mixed + x_ref[...]
    o_ref[...] = _swiglu_residual(x, gain_ref, wg_ref, wu_ref, wd_ref, ff_tile)


def _ffn_tiles(n_tokens, d_ff):
    row_tile = 512 if n_tokens % 512 == 0 else n_tokens
    ff_tile = d_ff // 2 if (d_ff // 2) % LANES == 0 else d_ff
    return row_tile, ff_tile


def _ffn_call(x, gain, wg, wu, wd, proj=None):
    n, d = x.shape
    d_ff = wg.shape[1]
    row_tile, ff_tile = _ffn_tiles(n, d_ff)
    row_spec = pl.BlockSpec((row_tile, d), lambda i: (i, 0))
    weight_specs = [_const_spec((1, d)), _const_spec((d, d_ff)), _const_spec((d, d_ff)),
                    _const_spec((d_ff, d))]
    if proj is None:
        body = functools.partial(_ffn_kernel, ff_tile=ff_tile)
        in_specs = [row_spec] + weight_specs
        args = (x, gain, wg, wu, wd)
        name = "ffn1"
    else:
        o_sb, o_rw, w_out = proj
        body = functools.partial(_proj_ffn_kernel, ff_tile=ff_tile)
        in_specs = [row_spec,
                    pl.BlockSpec((row_tile, SB_WIDTH), lambda i: (i, 0)),
                    pl.BlockSpec((row_tile, RW_WIDTH), lambda i: (i, 0)),
                    _const_spec(w_out.shape)] + weight_specs
        args = (x, o_sb, o_rw, w_out, gain, wg, wu, wd)
        name = "ffn2"
    return pl.pallas_call(
        body,
        out_shape=jax.ShapeDtypeStruct((n, d), F32),
        grid=(n // row_tile,),
        in_specs=in_specs,
        out_specs=row_spec,
        compiler_params=pltpu.CompilerParams(
            dimension_semantics=("parallel",), vmem_limit_bytes=VMEM_LIMIT),
        name=name,
    )(*args)


def _mix_in_kernel(x_ref, gain_ref, w_ref, qg_ref, kg_ref, mean_bd_ref,
                   q_ref, k_ref, v_ref, rw_ref):
    h = _rms_norm_rows(x_ref[...], gain_ref[...]).astype(BF16)
    mean_bd = mean_bd_ref[...]

    def head_norm(t, gain):
        ms = _head_reduce(t * t, mean_bd, two_pass=False)
        return t * lax.rsqrt(ms + RMS_EPS) * gain

    q = jnp.dot(h, w_ref[:, 0:SB_WIDTH], preferred_element_type=F32)
    q_ref[...] = (head_norm(q, qg_ref[...]) * (HEAD_DIM ** -0.5)).astype(BF16)
    k = jnp.dot(h, w_ref[:, SB_WIDTH:2 * SB_WIDTH], preferred_element_type=F32)
    k_ref[...] = head_norm(k, kg_ref[...]).astype(BF16)
    v = jnp.dot(h, w_ref[:, 2 * SB_WIDTH:3 * SB_WIDTH], preferred_element_type=F32)
    v_ref[...] = v.astype(BF16)
    rw_ref[...] = jnp.dot(h, w_ref[:, 3 * SB_WIDTH:], preferred_element_type=F32)


def _mix_in_call(x, gain, w_in, q_gain, k_gain, mean_bd):
    n, d = x.shape
    row_tile = 512 if n % 512 == 0 else n
    row = lambda width: pl.BlockSpec((row_tile, width), lambda i: (i, 0))
    return pl.pallas_call(
        _mix_in_kernel,
        out_shape=(jax.ShapeDtypeStruct((n, SB_WIDTH), BF16),
                   jax.ShapeDtypeStruct((n, SB_WIDTH), BF16),
                   jax.ShapeDtypeStruct((n, SB_WIDTH), BF16),
                   jax.ShapeDtypeStruct((n, RW_PROJ), F32)),
        grid=(n // row_tile,),
        in_specs=[row(d), _const_spec((1, d)), _const_spec(w_in.shape),
                  _const_spec((1, SB_WIDTH)), _const_spec((1, SB_WIDTH)),
                  _const_spec(mean_bd.shape)],
        out_specs=(row(SB_WIDTH), row(SB_WIDTH), row(SB_WIDTH), row(RW_PROJ)),
        compiler_params=pltpu.CompilerParams(
            dimension_semantics=("parallel",), vmem_limit_bytes=VMEM_LIMIT),
        name="mix_in",
    )(x, gain, w_in, q_gain, k_gain, mean_bd)


def _sb_attn_kernel(q_ref, k_ref, v_ref, gain_ref, o_ref):
    blk = pl.program_id(2)
    q = q_ref[...]
    lane = lax.broadcasted_iota(jnp.int32, (SB_BLOCK, LANES), 1)
    row = lax.broadcasted_iota(jnp.int32, (SB_BLOCK, LANES), 0)
    first_head = lane < HEAD_DIM
    zero = jnp.zeros_like(q)
    q_heads = (jnp.where(first_head, q, zero), jnp.where(first_head, zero, q))
    causal = lane < row
    rr = lax.broadcasted_iota(jnp.int32, (SB_BLOCK, 2 * LANES), 0)
    cc = lax.broadcasted_iota(jnp.int32, (SB_BLOCK, 2 * LANES), 1)
    suffix = jnp.where((rr > cc) | (cc >= LANES), 1.0, 0.0).astype(BF16)

    def key_block(j, carry, diagonal):
        start = pl.multiple_of(j * SB_BLOCK, SB_BLOCK)
        kb = k_ref[pl.ds(start, SB_BLOCK), :]
        vb = v_ref[pl.ds(start, SB_BLOCK), :]
        new = []
        for qh, (later, acc) in zip(q_heads, carry):
            z = lax.dot_general(qh, kb, (((1,), (1,)), ((), ())), preferred_element_type=F32)
            soft = jnp.log(1.0 + jnp.exp(-jnp.abs(z)))
            log_beta = jnp.minimum(z, 0.0) - soft
            log_1m_beta = -jnp.maximum(z, 0.0) - soft
            if diagonal:
                log_1m_beta = jnp.where(causal, log_1m_beta, 0.0)
            sums = jnp.dot(log_1m_beta.astype(BF16), suffix, preferred_element_type=F32)
            att = jnp.exp(log_beta + sums[:, :LANES] + later)
            if diagonal:
                att = jnp.where(causal, att, 0.0)
            acc = acc + jnp.dot(att.astype(BF16), vb, preferred_element_type=F32)
            new.append((later + sums[:, LANES:], acc))
        return tuple(new)

    zeros = jnp.zeros((SB_BLOCK, LANES), F32)
    carry = key_block(blk, ((zeros, zeros), (zeros, zeros)), diagonal=True)
    carry = lax.fori_loop(0, blk, lambda t, c: key_block(blk - 1 - t, c, diagonal=False), carry)
    out = jnp.where(first_head, carry[0][1], carry[1][1])
    sq = out * out
    ms0 = jnp.sum(jnp.where(first_head, sq, 0.0), axis=1, keepdims=True)
    ms1 = jnp.sum(jnp.where(first_head, 0.0, sq), axis=1, keepdims=True)
    ms = jnp.where(first_head, ms0, ms1) * (1.0 / HEAD_DIM)
    o_ref[...] = (out * lax.rsqrt(ms + RMS_EPS) * gain_ref[...]).astype(o_ref.dtype)


def _sb_attn_call(q, k, v, out_gain):
    b, s, width = q.shape
    pairs = width // LANES
    blk_spec = pl.BlockSpec((None, SB_BLOCK, LANES), lambda bi, p, i: (bi, i, p))
    seq_spec = pl.BlockSpec((None, s, LANES), lambda bi, p, i: (bi, 0, p))
    return pl.pallas_call(
        _sb_attn_kernel,
        out_shape=jax.ShapeDtypeStruct((b, s, width), BF16),
        grid=(b, pairs, s // SB_BLOCK),
        in_specs=[blk_spec, seq_spec, seq_spec,
                  pl.BlockSpec((1, LANES), lambda bi, p, i: (0, p))],
        out_specs=blk_spec,
        compiler_params=pltpu.CompilerParams(
            dimension_semantics=("parallel", "parallel", "arbitrary"),
            vmem_limit_bytes=VMEM_LIMIT),
        name="sb_attn",
    )(q, k, v, out_gain)


def _rwkv_kernel(u_ref, mu_ref, w0_ref, w2_ref, a0_ref, a2_ref, g2_ref, kk_ref, ka_ref,
                 rk_ref, lnw_ref, lnb_ref, ones_bd_ref, tri_ref, o_ref,
                 prev_ref, state_ref, r_s, kf_s, v_s, kn_s, eta_s, ld_s, y_s):
    n_rows = u_ref.shape[0]

    @pl.when(pl.program_id(1) == 0)
    def _():
        prev_ref[...] = jnp.zeros_like(prev_ref)
        state_ref[...] = jnp.zeros_like(state_ref)

    u = u_ref[...]
    row_id = lax.broadcasted_iota(jnp.int32, u.shape, 0)
    shifted = jnp.where(row_id == 0, prev_ref[...], pltpu.roll(u, 1, axis=0))
    prev_ref[...] = u[n_rows - 1:n_rows, :]
    u = u + (shifted - u) * mu_ref[...]

    x_r = u[:, 0:RW_WIDTH]
    x_k = u[:, RW_WIDTH:2 * RW_WIDTH]
    x_v = u[:, 2 * RW_WIDTH:3 * RW_WIDTH]
    x_wa = u[:, 3 * RW_WIDTH:3 * RW_WIDTH + DECAY_LORA + AAA_LORA]
    x_g = u[:, 3 * RW_WIDTH + DECAY_LORA + AAA_LORA:]
    ones_bd = ones_bd_ref[...]

    pre = -(w0_ref[...] + _dot_f32(jnp.tanh(x_wa), w2_ref[...]))
    log_w = -(jnp.maximum(pre, 0.0) + jnp.log(1.0 + jnp.exp(-jnp.abs(pre)))) - 0.5
    ld_s[...] = -jnp.exp(log_w)
    eta = jax.nn.sigmoid(a0_ref[...] + _dot(x_wa, a2_ref[...]))
    gate = _dot(jax.nn.sigmoid(x_g), g2_ref[...])
    kk = x_k * kk_ref[...]
    kk_norm = jnp.sqrt(_head_reduce(kk * kk, ones_bd, two_pass=True))
    kf = x_k * (1.0 + (eta - 1.0) * ka_ref[...])
    bonus = _head_reduce(x_r * kf * rk_ref[...], ones_bd, two_pass=True) * x_v
    r_s[...] = x_r
    kf_s[...] = kf
    v_s[...] = x_v
    kn_s[...] = kk / jnp.maximum(kk_norm, KK_NORM_FLOOR)
    eta_s[...] = eta

    lane = lax.broadcasted_iota(jnp.int32, (RW_CHUNK, LANES), 1)
    first_head = lane < HEAD_DIM
    ri = lax.broadcasted_iota(jnp.int32, (LANES, LANES), 0)
    ci = lax.broadcasted_iota(jnp.int32, (LANES, LANES), 1)
    same_head = (ri < HEAD_DIM) == (ci < HEAD_DIM)
    strict_lower = same_head & (ci < ri)
    lower = same_head & (ci <= ri)
    eye = (ri == ci).astype(F32)
    tri = tri_ref[...]

    def stack_heads(t):
        return jnp.concatenate([jnp.where(first_head, t, 0.0), jnp.where(first_head, 0.0, t)],
                               axis=0)

    def chunk(c, _):
        rows = pl.ds(pl.multiple_of(c * RW_CHUNK, RW_CHUNK), RW_CHUNK)
        ld = ld_s[rows, :]
        cum = _dot_f32(tri, ld)
        total = cum[RW_CHUNK - 1:RW_CHUNK, :]
        e_incl = jnp.exp(cum)
        e_excl = jnp.exp(cum - ld)
        e_inv = jnp.exp(-cum)
        e_tail = jnp.exp(total - cum)
        w_total = jnp.exp(total)
        kn = kn_s[rows, :]
        kf_c = kf_s[rows, :]
        b = kn * eta_s[rows, :]
        a_dec = -kn * e_excl
        r_dec = r_s[rows, :] * e_incl
        b_inv = b * e_inv
        k_inv = kf_c * e_inv
        b_tail = b * e_tail
        k_tail = kf_c * e_tail
        v_c = v_s[rows, :]
        for p in range(RW_WIDTH // LANES):
            cols = slice(p * LANES, (p + 1) * LANES)
            a_sm = stack_heads(a_dec[:, cols])
            r_sm = stack_heads(r_dec[:, cols])
            v_sm = stack_heads(v_c[:, cols])
            bi, ki = b_inv[:, cols], k_inv[:, cols]
            scores = _dot_nt(jnp.concatenate([a_sm, r_sm], axis=0),
                             jnp.concatenate([bi, bi, ki, ki], axis=0))
            l_ab = jnp.where(strict_lower, scores[:LANES, :LANES], 0.0)
            l_ak = jnp.where(strict_lower, scores[:LANES, LANES:], 0.0)
            m_rb = jnp.where(lower, scores[LANES:, :LANES], 0.0)
            m_rk = jnp.where(lower, scores[LANES:, LANES:], 0.0)
            inv = eye + l_ab
            power = l_ab
            for _ in range(5):
                power = _dot(power, power)
                inv = inv + _dot(inv, power)
            pu = _dot(inv, jnp.concatenate([a_sm, _dot(l_ak, v_sm)], axis=1))
            qy = _dot(m_rb, pu)
            q_sm = r_sm + qy[:, :LANES]
            y0_sm = qy[:, LANES:] + _dot(m_rk, v_sm)
            gh = _dot(stack_heads(b_tail[:, cols]).T, pu)
            g_bd = gh[:, :LANES] + eye * w_total[:, cols]
            h_bd = gh[:, LANES:] + _dot(stack_heads(k_tail[:, cols]).T, v_sm)
            state = state_ref[p]
            y_sm = _dot(q_sm, state) + y0_sm
            state_ref[p] = _dot(g_bd, state) + h_bd
            y_s[rows, cols] = y_sm[:RW_CHUNK, :] + y_sm[RW_CHUNK:, :]
        return 0

    lax.fori_loop(0, n_rows // RW_CHUNK, chunk, 0)

    y = y_s[...]
    centered = y - _head_reduce(y, ones_bd, two_pass=True) * (1.0 / HEAD_DIM)
    var = _head_reduce(centered * centered, ones_bd, two_pass=True) * (1.0 / HEAD_DIM)
    yn = centered * lax.rsqrt(var + LNX_EPS) * lnw_ref[...] + lnb_ref[...]
    o_ref[...] = ((yn + bonus) * gate).astype(o_ref.dtype)


def _rwkv_call(rw_in, mu, w0, w2_pad, a0, a2_pad, g2, k_k, k_a, r_k, ln_w, ln_b, ones_bd, tri):
    b, s, width = rw_in.shape
    row_tile = 256 if s % 256 == 0 else s
    vec = lambda n: _const_spec((1, n))
    scratch_rows = pltpu.VMEM((row_tile, RW_WIDTH), F32)
    return pl.pallas_call(
        _rwkv_kernel,
        out_shape=jax.ShapeDtypeStruct((b, s, RW_WIDTH), BF16),
        grid=(b, s // row_tile),
        in_specs=[pl.BlockSpec((None, row_tile, width), lambda bi, j: (bi, j, 0)),
                  vec(width), vec(RW_WIDTH), _const_spec(w2_pad.shape), vec(RW_WIDTH),
                  _const_spec(a2_pad.shape), _const_spec(g2.shape), vec(RW_WIDTH),
                  vec(RW_WIDTH), vec(RW_WIDTH), vec(RW_WIDTH), vec(RW_WIDTH),
                  _const_spec(ones_bd.shape), _const_spec(tri.shape)],
        out_specs=pl.BlockSpec((None, row_tile, RW_WIDTH), lambda bi, j: (bi, j, 0)),
        scratch_shapes=[pltpu.VMEM((1, width), F32),
                        pltpu.VMEM((RW_WIDTH // LANES, LANES, LANES), F32)]
                       + [scratch_rows] * 7,
        compiler_params=pltpu.CompilerParams(
            dimension_semantics=("parallel", "arbitrary"), vmem_limit_bytes=VMEM_LIMIT),
        name="rwkv7",
    )(rw_in, mu, w0, w2_pad, a0, a2_pad, g2, k_k, k_a, r_k, ln_w, ln_b, ones_bd, tri)


def _head_block_diag(value, dtype):
    idx = jnp.arange(MXU_DIM) // HEAD_DIM
    return jnp.where(idx[:, None] == idx[None, :], value, 0.0).astype(dtype)


def kernel(x, norm_ffn1, ffn1_gate, ffn1_up, ffn1_down, norm_mix, w_in, sb_q_norm, sb_k_norm,
           sb_out_norm, rw_mu, rw_w0, rw_w2, rw_a0, rw_a2, rw_g2, rw_k_k, rw_k_a, rw_r_k,
           rw_ln_w, rw_ln_b, w_out, norm_ffn2, ffn2_gate, ffn2_up, ffn2_down):
    b, s, d = x.shape
    n = b * s
    heads = SB_WIDTH // HEAD_DIM
    mean_bd = _head_block_diag(1.0 / HEAD_DIM, BF16)
    ones_bd = _head_block_diag(1.0, BF16)
    tri = jnp.tril(jnp.ones((RW_CHUNK, RW_CHUNK), F32))
    row = lambda t: t.reshape(1, -1)

    h = x.reshape(n, d)
    for l in range(norm_ffn1.shape[0]):
        h = _ffn_call(h, row(norm_ffn1[l]), ffn1_gate[l].astype(BF16), ffn1_up[l].astype(BF16),
                      ffn1_down[l].astype(BF16))

        q, k, v, rw_in = _mix_in_call(
            h, row(norm_mix[l]), w_in[l].astype(BF16),
            row(jnp.tile(sb_q_norm[l], heads)), row(jnp.tile(sb_k_norm[l], heads)), mean_bd)

        o_sb = _sb_attn_call(q.reshape(b, s, SB_WIDTH), k.reshape(b, s, SB_WIDTH),
                             v.reshape(b, s, SB_WIDTH), row(sb_out_norm[l]))

        zeros = jnp.zeros((DECAY_LORA, RW_WIDTH), F32)
        w2_pad = jnp.concatenate([rw_w2[l], zeros], axis=0)
        a2_pad = jnp.concatenate([zeros, rw_a2[l]], axis=0).astype(BF16)
        o_rw = _rwkv_call(rw_in.reshape(b, s, RW_PROJ), row(rw_mu[l]), row(rw_w0[l]), w2_pad,
                          row(rw_a0[l]), a2_pad, rw_g2[l].astype(BF16), row(rw_k_k[l]),
                          row(rw_k_a[l]), row(rw_r_k[l]), row(rw_ln_w[l]), row(rw_ln_b[l]),
                          ones_bd, tri)

        h = _ffn_call(h, row(norm_ffn2[l]), ffn2_gate[l].astype(BF16), ffn2_up[l].astype(BF16),
                      ffn2_down[l].astype(BF16),
                      proj=(o_sb.reshape(n, SB_WIDTH), o_rw.reshape(n, RW_WIDTH),
                            w_out[l].astype(BF16)))
    return h.reshape(b, s, d)
```

```python
import functools

import jax
import jax.numpy as jnp
from jax import lax
from jax.experimental import pallas as pl
from jax.experimental.pallas import tpu as pltpu

F32 = jnp.float32
BF16 = jnp.bfloat16

HEAD_DIM = 64
SB_WIDTH = 512
RW_WIDTH = 512
DECAY_LORA = 64
AAA_LORA = 64
GATE_LORA = 128
RW_PROJ = 3 * RW_WIDTH + DECAY_LORA + AAA_LORA + GATE_LORA
RMS_EPS = 1e-6
LNX_EPS = 64e-5
KK_NORM_FLOOR = 1e-12

LANES = 128
MXU_DIM = 256
RW_CHUNK = 64
SB_TILE = 256
VMEM_LIMIT = 56 * 1024 * 1024


def _const_spec(shape):
    zeros = (0,) * len(shape)
    return pl.BlockSpec(shape, lambda *_: zeros, pipeline_mode=pl.Buffered(1))


def _dot(a, b):
    return jnp.dot(a.astype(BF16), b.astype(BF16), preferred_element_type=F32)


def _dot_nt(a, b):
    return lax.dot_general(a.astype(BF16), b.astype(BF16), (((1,), (1,)), ((), ())),
                           preferred_element_type=F32)


def _dot_f32(a, b):
    return jnp.dot(a, b, precision=lax.Precision.HIGHEST, preferred_element_type=F32)


def _rms_norm_rows(x, gain):
    ms = jnp.mean(x * x, axis=-1, keepdims=True)
    return x * lax.rsqrt(ms + RMS_EPS) * gain


def _head_reduce(t, bd, two_pass):
    outs = []
    for half in range(t.shape[1] // MXU_DIM):
        th = t[:, half * MXU_DIM:(half + 1) * MXU_DIM]
        hi = th.astype(BF16)
        red = jnp.dot(hi, bd, preferred_element_type=F32)
        if two_pass:
            lo = (th - hi.astype(F32)).astype(BF16)
            red = red + jnp.dot(lo, bd, preferred_element_type=F32)
        outs.append(red)
    return jnp.concatenate(outs, axis=1)


def _swiglu_residual(x, gain_ref, wg_ref, wu_ref, wd_ref, ff_tile):
    h = _rms_norm_rows(x, gain_ref[...]).astype(BF16)
    acc = x
    for c in range(wg_ref.shape[1] // ff_tile):
        cols = slice(c * ff_tile, (c + 1) * ff_tile)
        gate = jnp.dot(h, wg_ref[:, cols], preferred_element_type=F32)
        up = jnp.dot(h, wu_ref[:, cols], preferred_element_type=F32)
        act = (gate * jax.nn.sigmoid(gate) * up * 0.5).astype(BF16)
        acc = acc + jnp.dot(act, wd_ref[cols, :], preferred_element_type=F32)
    return acc


def _ffn_kernel(x_ref, gain_ref, wg_ref, wu_ref, wd_ref, o_ref, *, ff_tile):
    o_ref[...] = _swiglu_residual(x_ref[...], gain_ref, wg_ref, wu_ref, wd_ref, ff_tile)


def _proj_ffn_kernel(x_ref, osb_ref, orw_ref, wo_ref, gain_ref, wg_ref, wu_ref, wd_ref,
                     o_ref, *, ff_tile):
    mixed = (jnp.dot(osb_ref[...], wo_ref[:SB_WIDTH, :], preferred_element_type=F32)
             + jnp.dot(orw_ref[...], wo_ref[SB_WIDTH:, :], preferred_element_type=F32))
    x = mixed + x_ref[...]
    o_ref[...] = _swiglu_residual(x, gain_ref, wg_ref, wu_ref, wd_ref, ff_tile)


def _ffn_tiles(n_tokens, d_ff):
    row_tile = 512 if n_tokens % 512 == 0 else n_tokens
    ff_tile = d_ff // 2 if (d_ff // 2) % LANES == 0 else d_ff
    return row_tile, ff_tile


def _ffn_call(x, gain, wg, wu, wd, proj=None):
    n, d = x.shape
    d_ff = wg.shape[1]
    row_tile, ff_tile = _ffn_tiles(n, d_ff)
    row_spec = pl.BlockSpec((row_tile, d), lambda i: (i, 0))
    weight_specs = [_const_spec((1, d)), _const_spec((d, d_ff)), _const_spec((d, d_ff)),
                    _const_spec((d_ff, d))]
    if proj is None:
        body = functools.partial(_ffn_kernel, ff_tile=ff_tile)
        in_specs = [row_spec] + weight_specs
        args = (x, gain, wg, wu, wd)
        name = "ffn1"
    else:
        o_sb, o_rw, w_out = proj
        body = functools.partial(_proj_ffn_kernel, ff_tile=ff_tile)
        in_specs = [row_spec,
                    pl.BlockSpec((row_tile, SB_WIDTH), lambda i: (i, 0)),
                    pl.BlockSpec((row_tile, RW_WIDTH), lambda i: (i, 0)),
                    _const_spec(w_out.shape)] + weight_specs
        args = (x, o_sb, o_rw, w_out, gain, wg, wu, wd)
        name = "ffn2"
    return pl.pallas_call(
        body,
        out_shape=jax.ShapeDtypeStruct((n, d), F32),
        grid=(n // row_tile,),
        in_specs=in_specs,
        out_specs=row_spec,
        compiler_params=pltpu.CompilerParams(
            dimension_semantics=("parallel",), vmem_limit_bytes=VMEM_LIMIT),
        name=name,
    )(*args)


def _mix_in_kernel(x_ref, gain_ref, w_ref, wvt_ref, qg_ref, kg_ref, mean_bd_ref,
                   q_ref, k_ref, vt_ref, rw_ref):
    h = _rms_norm_rows(x_ref[...], gain_ref[...]).astype(BF16)
    mean_bd = mean_bd_ref[...]

    def head_norm(t, gain):
        ms = _head_reduce(t * t, mean_bd, two_pass=False)
        return t * lax.rsqrt(ms + RMS_EPS) * gain

    q = jnp.dot(h, w_ref[:, 0:SB_WIDTH], preferred_element_type=F32)
    q_ref[...] = (head_norm(q, qg_ref[...]) * (HEAD_DIM ** -0.5)).astype(BF16)
    k = jnp.dot(h, w_ref[:, SB_WIDTH:2 * SB_WIDTH], preferred_element_type=F32)
    k_ref[...] = head_norm(k, kg_ref[...]).astype(BF16)
    vt = _dot_nt(wvt_ref[...], h).astype(BF16)
    for t in range(vt_ref.shape[0]):
        vt_ref[t] = vt[:, t * SB_TILE:(t + 1) * SB_TILE]
    rw_ref[...] = jnp.dot(h, w_ref[:, 3 * SB_WIDTH:], preferred_element_type=F32)


def _mix_in_call(x, gain, w_in, wv_t, q_gain, k_gain, mean_bd):
    n, d = x.shape
    row_tile = 2 * SB_TILE
    assert n % row_tile == 0
    row = lambda width: pl.BlockSpec((row_tile, width), lambda i: (i, 0))
    return pl.pallas_call(
        _mix_in_kernel,
        out_shape=(jax.ShapeDtypeStruct((n, SB_WIDTH), BF16),
                   jax.ShapeDtypeStruct((n, SB_WIDTH), BF16),
                   jax.ShapeDtypeStruct((n // SB_TILE, SB_WIDTH, SB_TILE), BF16),
                   jax.ShapeDtypeStruct((n, RW_PROJ), F32)),
        grid=(n // row_tile,),
        in_specs=[row(d), _const_spec((1, d)), _const_spec(w_in.shape),
                  _const_spec(wv_t.shape), _const_spec((1, SB_WIDTH)),
                  _const_spec((1, SB_WIDTH)), _const_spec(mean_bd.shape)],
        out_specs=(row(SB_WIDTH), row(SB_WIDTH),
                   pl.BlockSpec((row_tile // SB_TILE, SB_WIDTH, SB_TILE), lambda i: (i, 0, 0)),
                   row(RW_PROJ)),
        compiler_params=pltpu.CompilerParams(
            dimension_semantics=("parallel",), vmem_limit_bytes=VMEM_LIMIT),
        name="mix_in",
    )(x, gain, w_in, wv_t, q_gain, k_gain, mean_bd)


def _sb_attn_kernel(q_ref, k_ref, vt_ref, gain_ref, o_ref, later_ref, acc_ref, z_ref):
    tile = pl.program_id(2)
    q = q_ref[...]
    lane = lax.broadcasted_iota(jnp.int32, q.shape, 1)
    first_head = lane < HEAD_DIM
    zero = jnp.zeros_like(q)
    q_heads = jnp.concatenate([jnp.where(first_head, q, zero), jnp.where(first_head, zero, q)],
                              axis=0)
    key_pos = lax.broadcasted_iota(jnp.int32, (SB_TILE, 2 * SB_TILE), 0)
    col = lax.broadcasted_iota(jnp.int32, (SB_TILE, 2 * SB_TILE), 1)
    causal = key_pos < jnp.where(col >= SB_TILE, col - SB_TILE, col)
    kk = lax.broadcasted_iota(jnp.int32, (SB_TILE, SB_TILE), 0)
    jj = lax.broadcasted_iota(jnp.int32, (SB_TILE, SB_TILE), 1)
    later_keys = jnp.where(jj > kk, 1.0, 0.0).astype(BF16)

    def scores(j):
        start = pl.multiple_of(jnp.maximum(j, 0) * SB_TILE, SB_TILE)
        return lax.dot_general(k_ref[pl.ds(start, SB_TILE), :], q_heads,
                               (((1,), (1,)), ((), ())), preferred_element_type=F32)

    def log_betas(z, diagonal):
        neg_part = jnp.minimum(z, 0.0)
        neg_relu = neg_part - z
        soft = jnp.log(1.0 + jnp.exp(neg_part + neg_relu))
        log_1m_beta = neg_relu - soft
        if diagonal:
            log_1m_beta = jnp.where(causal, log_1m_beta, 0.0)
        return neg_part - soft, log_1m_beta

    def key_tiles(tiles, logits, diagonal):
        parts = [log_betas(z, diagonal) for z in logits]
        later = later_ref[...]
        acc0 = acc_ref[:HEAD_DIM, :]
        acc1 = acc_ref[HEAD_DIM:, :]
        for j, (log_beta, log_1m_beta) in zip(tiles, parts):
            v_t = vt_ref[j]
            sums = jnp.dot(later_keys, log_1m_beta.astype(BF16), preferred_element_type=F32)
            att = jnp.exp(log_beta + sums + later)
            if diagonal:
                att = jnp.where(causal, att, 0.0)
            att = att.astype(BF16)
            acc0 = acc0 + jnp.dot(v_t[:HEAD_DIM, :], att[:, :SB_TILE],
                                  preferred_element_type=F32)
            acc1 = acc1 + jnp.dot(v_t[HEAD_DIM:, :], att[:, SB_TILE:],
                                  preferred_element_type=F32)
            later = later + jnp.sum(log_1m_beta, axis=0, keepdims=True)
        later_ref[...] = later
        acc_ref[:HEAD_DIM, :] = acc0
        acc_ref[HEAD_DIM:, :] = acc1

    later_ref[...] = jnp.zeros_like(later_ref)
    acc_ref[...] = jnp.zeros_like(acc_ref)
    n_pairs = tile // 2
    z_diag = scores(tile)
    z_ref[0, 0] = scores(tile - 1)
    z_ref[0, 1] = scores(tile - 2)
    key_tiles([tile], [z_diag], diagonal=True)

    def tile_pair(t, _):
        slot = t % 2
        j = tile - 1 - 2 * t
        logits = [z_ref[slot, 0], z_ref[slot, 1]]
        z_ref[1 - slot, 0] = scores(j - 2)
        z_ref[1 - slot, 1] = scores(j - 3)
        key_tiles([j, j - 1], logits, diagonal=False)
        return 0

    lax.fori_loop(0, n_pairs, tile_pair, 0)

    @pl.when(tile % 2 == 1)
    def _():
        key_tiles([0], [z_ref[n_pairs % 2, 0]], diagonal=False)

    def head_norm(acc):
        ms = jnp.sum(acc * acc, axis=0, keepdims=True) * (1.0 / HEAD_DIM)
        return acc * lax.rsqrt(ms + RMS_EPS)

    out_t = jnp.concatenate([head_norm(acc_ref[:HEAD_DIM, :]), head_norm(acc_ref[HEAD_DIM:, :])],
                            axis=0)
    o_ref[...] = (out_t.T * gain_ref[...]).astype(o_ref.dtype)


def _sb_attn_call(q, k, v_t, out_gain):
    b, s, width = q.shape
    pairs = width // LANES
    tiles = s // SB_TILE
    blk_spec = pl.BlockSpec((None, SB_TILE, LANES), lambda bi, p, i: (bi, i, p))
    return pl.pallas_call(
        _sb_attn_kernel,
        out_shape=jax.ShapeDtypeStruct((b, s, width), BF16),
        grid=(b, pairs, tiles),
        in_specs=[blk_spec,
                  pl.BlockSpec((None, s, LANES), lambda bi, p, i: (bi, 0, p)),
                  pl.BlockSpec((tiles, LANES, SB_TILE), lambda bi, p, i: (bi, p, 0)),
                  pl.BlockSpec((1, LANES), lambda bi, p, i: (0, p))],
        out_specs=blk_spec,
        scratch_shapes=[pltpu.VMEM((1, 2 * SB_TILE), F32),
                        pltpu.VMEM((LANES, SB_TILE), F32),
                        pltpu.VMEM((2, 2, SB_TILE, 2 * SB_TILE), F32)],
        compiler_params=pltpu.CompilerParams(
            dimension_semantics=("parallel", "parallel", "arbitrary"),
            vmem_limit_bytes=VMEM_LIMIT),
        name="sb_attn",
    )(q, k, v_t, out_gain)


def _rwkv_kernel(u_ref, mu_ref, w0_ref, w2_ref, a0_ref, a2_ref, g2_ref, kk_ref, ka_ref,
                 rk_ref, lnw_ref, lnb_ref, ones_bd_ref, tri_ref, o_ref,
                 prev_ref, state_ref, r_s, kf_s, v_s, kn_s, eta_s, ld_s, y_s):
    n_rows = u_ref.shape[0]

    @pl.when(pl.program_id(1) == 0)
    def _():
        prev_ref[...] = jnp.zeros_like(prev_ref)
        state_ref[...] = jnp.zeros_like(state_ref)

    u = u_ref[...]
    row_id = lax.broadcasted_iota(jnp.int32, u.shape, 0)
    shifted = jnp.where(row_id == 0, prev_ref[...], pltpu.roll(u, 1, axis=0))
    prev_ref[...] = u[n_rows - 1:n_rows, :]
    u = u + (shifted - u) * mu_ref[...]

    x_r = u[:, 0:RW_WIDTH]
    x_k = u[:, RW_WIDTH:2 * RW_WIDTH]
    x_v = u[:, 2 * RW_WIDTH:3 * RW_WIDTH]
    x_wa = u[:, 3 * RW_WIDTH:3 * RW_WIDTH + DECAY_LORA + AAA_LORA]
    x_g = u[:, 3 * RW_WIDTH + DECAY_LORA + AAA_LORA:]
    ones_bd = ones_bd_ref[...]

    pre = -(w0_ref[...] + _dot_f32(jnp.tanh(x_wa), w2_ref[...]))
    log_w = -(jnp.maximum(pre, 0.0) + jnp.log(1.0 + jnp.exp(-jnp.abs(pre)))) - 0.5
    ld_s[...] = -jnp.exp(log_w)
    eta = jax.nn.sigmoid(a0_ref[...] + _dot(x_wa, a2_ref[...]))
    gate = _dot(jax.nn.sigmoid(x_g), g2_ref[...])
    kk = x_k * kk_ref[...]
    kk_norm = jnp.sqrt(_head_reduce(kk * kk, ones_bd, two_pass=True))
    kf = x_k * (1.0 + (eta - 1.0) * ka_ref[...])
    bonus = _head_reduce(x_r * kf * rk_ref[...], ones_bd, two_pass=True) * x_v
    r_s[...] = x_r
    kf_s[...] = kf
    v_s[...] = x_v
    kn_s[...] = kk / jnp.maximum(kk_norm, KK_NORM_FLOOR)
    eta_s[...] = eta

    lane = lax.broadcasted_iota(jnp.int32, (RW_CHUNK, LANES), 1)
    first_head = lane < HEAD_DIM
    ri = lax.broadcasted_iota(jnp.int32, (LANES, LANES), 0)
    ci = lax.broadcasted_iota(jnp.int32, (LANES, LANES), 1)
    same_head = (ri < HEAD_DIM) == (ci < HEAD_DIM)
    strict_lower = same_head & (ci < ri)
    lower = same_head & (ci <= ri)
    eye = (ri == ci).astype(F32)
    tri = tri_ref[...]

    def stack_heads(t):
        return jnp.concatenate([jnp.where(first_head, t, 0.0), jnp.where(first_head, 0.0, t)],
                               axis=0)

    def chunk(c, _):
        rows = pl.ds(pl.multiple_of(c * RW_CHUNK, RW_CHUNK), RW_CHUNK)
        ld = ld_s[rows, :]
        cum = _dot_f32(tri, ld)
        total = cum[RW_CHUNK - 1:RW_CHUNK, :]
        e_incl = jnp.exp(cum)
        e_excl = jnp.exp(cum - ld)
        e_inv = jnp.exp(-cum)
        e_tail = jnp.exp(total - cum)
        w_total = jnp.exp(total)
        kn = kn_s[rows, :]
        kf_c = kf_s[rows, :]
        b = kn * eta_s[rows, :]
        a_dec = -kn * e_excl
        r_dec = r_s[rows, :] * e_incl
        b_inv = b * e_inv
        k_inv = kf_c * e_inv
        b_tail = b * e_tail
        k_tail = kf_c * e_tail
        v_c = v_s[rows, :]
        for p in range(RW_WIDTH // LANES):
            cols = slice(p * LANES, (p + 1) * LANES)
            a_sm = stack_heads(a_dec[:, cols])
            r_sm = stack_heads(r_dec[:, cols])
            v_sm = stack_heads(v_c[:, cols])
            bi, ki = b_inv[:, cols], k_inv[:, cols]
            scores = _dot_nt(jnp.concatenate([a_sm, r_sm], axis=0),
                             jnp.concatenate([bi, bi, ki, ki], axis=0))
            l_ab = jnp.where(strict_lower, scores[:LANES, :LANES], 0.0)
            l_ak = jnp.where(strict_lower, scores[:LANES, LANES:], 0.0)
            m_rb = jnp.where(lower, scores[LANES:, :LANES], 0.0)
            m_rk = jnp.where(lower, scores[LANES:, LANES:], 0.0)
            inv = eye + l_ab
            power = l_ab
            for _ in range(5):
                power = _dot(power, power)
                inv = inv + _dot(inv, power)
            pu = _dot(inv, jnp.concatenate([a_sm, _dot(l_ak, v_sm)], axis=1))
            qy = _dot(m_rb, pu)
            q_sm = r_sm + qy[:, :LANES]
            y0_sm = qy[:, LANES:] + _dot(m_rk, v_sm)
            gh = _dot(stack_heads(b_tail[:, cols]).T, pu)
            g_bd = gh[:, :LANES] + eye * w_total[:, cols]
            h_bd = gh[:, LANES:] + _dot(stack_heads(k_tail[:, cols]).T, v_sm)
            state = state_ref[p]
            y_sm = _dot(q_sm, state) + y0_sm
            state_ref[p] = _dot(g_bd, state) + h_bd
            y_s[rows, cols] = y_sm[:RW_CHUNK, :] + y_sm[RW_CHUNK:, :]
        return 0

    lax.fori_loop(0, n_rows // RW_CHUNK, chunk, 0)

    y = y_s[...]
    centered = y - _head_reduce(y, ones_bd, two_pass=True) * (1.0 / HEAD_DIM)
    var = _head_reduce(centered * centered, ones_bd, two_pass=True) * (1.0 / HEAD_DIM)
    yn = centered * lax.rsqrt(var + LNX_EPS) * lnw_ref[...] + lnb_ref[...]
    o_ref[...] = ((yn + bonus) * gate).astype(o_ref.dtype)


def _rwkv_call(rw_in, mu, w0, w2_pad, a0, a2_pad, g2, k_k, k_a, r_k, ln_w, ln_b, ones_bd, tri):
    b, s, width = rw_in.shape
    row_tile = 256 if s % 256 == 0 else s
    vec = lambda n: _const_spec((1, n))
    scratch_rows = pltpu.VMEM((row_tile, RW_WIDTH), F32)
    return pl.pallas_call(
        _rwkv_kernel,
        out_shape=jax.ShapeDtypeStruct((b, s, RW_WIDTH), BF16),
        grid=(b, s // row_tile),
        in_specs=[pl.BlockSpec((None, row_tile, width), lambda bi, j: (bi, j, 0)),
                  vec(width), vec(RW_WIDTH), _const_spec(w2_pad.shape), vec(RW_WIDTH),
                  _const_spec(a2_pad.shape), _const_spec(g2.shape), vec(RW_WIDTH),
                  vec(RW_WIDTH), vec(RW_WIDTH), vec(RW_WIDTH), vec(RW_WIDTH),
                  _const_spec(ones_bd.shape), _const_spec(tri.shape)],
        out_specs=pl.BlockSpec((None, row_tile, RW_WIDTH), lambda bi, j: (bi, j, 0)),
        scratch_shapes=[pltpu.VMEM((1, width), F32),
                        pltpu.VMEM((RW_WIDTH // LANES, LANES, LANES), F32)]
                       + [scratch_rows] * 7,
        compiler_params=pltpu.CompilerParams(
            dimension_semantics=("parallel", "arbitrary"), vmem_limit_bytes=VMEM_LIMIT),
        name="rwkv7",
    )(rw_in, mu, w0, w2_pad, a0, a2_pad, g2, k_k, k_a, r_k, ln_w, ln_b, ones_bd, tri)


def _head_block_diag(value, dtype):
    idx = jnp.arange(MXU_DIM) // HEAD_DIM
    return jnp.where(idx[:, None] == idx[None, :], value, 0.0).astype(dtype)


def kernel(x, norm_ffn1, ffn1_gate, ffn1_up, ffn1_down, norm_mix, w_in, sb_q_norm, sb_k_norm,
           sb_out_norm, rw_mu, rw_w0, rw_w2, rw_a0, rw_a2, rw_g2, rw_k_k, rw_k_a, rw_r_k,
           rw_ln_w, rw_ln_b, w_out, norm_ffn2, ffn2_gate, ffn2_up, ffn2_down):
    b, s, d = x.shape
    n = b * s
    heads = SB_WIDTH // HEAD_DIM
    mean_bd = _head_block_diag(1.0 / HEAD_DIM, BF16)
    ones_bd = _head_block_diag(1.0, BF16)
    tri = jnp.tril(jnp.ones((RW_CHUNK, RW_CHUNK), F32))
    row = lambda t: t.reshape(1, -1)

    h = x.reshape(n, d)
    for l in range(norm_ffn1.shape[0]):
        h = _ffn_call(h, row(norm_ffn1[l]), ffn1_gate[l].astype(BF16), ffn1_up[l].astype(BF16),
                      ffn1_down[l].astype(BF16))

        w_in_l = w_in[l].astype(BF16)
        q, k, v_t, rw_in = _mix_in_call(
            h, row(norm_mix[l]), w_in_l, w_in_l[:, 2 * SB_WIDTH:3 * SB_WIDTH].T,
            row(jnp.tile(sb_q_norm[l], heads)), row(jnp.tile(sb_k_norm[l], heads)), mean_bd)

        o_sb = _sb_attn_call(q.reshape(b, s, SB_WIDTH), k.reshape(b, s, SB_WIDTH), v_t,
                             row(sb_out_norm[l]))

        zeros = jnp.zeros((DECAY_LORA, RW_WIDTH), F32)
        w2_pad = jnp.concatenate([rw_w2[l], zeros], axis=0)
        a2_pad = jnp.concatenate([zeros, rw_a2[l]], axis=0).astype(BF16)
        o_rw = _rwkv_call(rw_in.reshape(b, s, RW_PROJ), row(rw_mu[l]), row(rw_w0[l]), w2_pad,
                          row(rw_a0[l]), a2_pad, rw_g2[l].astype(BF16), row(rw_k_k[l]),
                          row(rw_k_a[l]), row(rw_r_k[l]), row(rw_ln_w[l]), row(rw_ln_b[l]),
                          ones_bd, tri)

        h = _ffn_call(h, row(norm_ffn2[l]), ffn2_gate[l].astype(BF16), ffn2_up[l].astype(BF16),
                      ffn2_down[l].astype(BF16),
                      proj=(o_sb.reshape(n, SB_WIDTH), o_rw.reshape(n, RW_WIDTH),
                            w_out[l].astype(BF16)))
    return h.reshape(b, s, d)
```

```python
import functools

import jax
import jax.numpy as jnp
from jax import lax
from jax.experimental import pallas as pl
from jax.experimental.pallas import tpu as pltpu

F32 = jnp.float32
BF16 = jnp.bfloat16

HEAD_DIM = 64
SB_WIDTH = 512
RW_WIDTH = 512
DECAY_LORA = 64
AAA_LORA = 64
GATE_LORA = 128
RW_PROJ = 3 * RW_WIDTH + DECAY_LORA + AAA_LORA + GATE_LORA
RMS_EPS = 1e-6
LNX_EPS = 64e-5
KK_NORM_FLOOR = 1e-12

LANES = 128
MXU_DIM = 256
RW_CHUNK = 64
RW_GROUP = 2
SB_TILE = 256
VMEM_LIMIT = 56 * 1024 * 1024


def _const_spec(shape):
    zeros = (0,) * len(shape)
    return pl.BlockSpec(shape, lambda *_: zeros, pipeline_mode=pl.Buffered(1))


def _dot(a, b):
    return jnp.dot(a.astype(BF16), b.astype(BF16), preferred_element_type=F32)


def _dot_nt(a, b):
    return lax.dot_general(a.astype(BF16), b.astype(BF16), (((1,), (1,)), ((), ())),
                           preferred_element_type=F32)


def _dot_f32(a, b):
    return jnp.dot(a, b, precision=lax.Precision.HIGHEST, preferred_element_type=F32)


def _prefix_sum(tri, x):
    hi = x.astype(BF16)
    rest = x - hi.astype(F32)
    mid = rest.astype(BF16)
    lo = (rest - mid.astype(F32)).astype(BF16)
    return (jnp.dot(tri, hi, preferred_element_type=F32)
            + jnp.dot(tri, mid, preferred_element_type=F32)
            + jnp.dot(tri, lo, preferred_element_type=F32))


def _rms_norm_rows(x, gain):
    ms = jnp.mean(x * x, axis=-1, keepdims=True)
    return x * lax.rsqrt(ms + RMS_EPS) * gain


def _head_reduce(t, bd, two_pass):
    outs = []
    for half in range(t.shape[1] // MXU_DIM):
        th = t[:, half * MXU_DIM:(half + 1) * MXU_DIM]
        hi = th.astype(BF16)
        red = jnp.dot(hi, bd, preferred_element_type=F32)
        if two_pass:
            lo = (th - hi.astype(F32)).astype(BF16)
            red = red + jnp.dot(lo, bd, preferred_element_type=F32)
        outs.append(red)
    return jnp.concatenate(outs, axis=1)


def _swiglu_residual(x, gain_ref, wg_ref, wu_ref, wd_ref, ff_tile):
    h = _rms_norm_rows(x, gain_ref[...]).astype(BF16)
    acc = x
    for c in range(wg_ref.shape[1] // ff_tile):
        cols = slice(c * ff_tile, (c + 1) * ff_tile)
        gate = jnp.dot(h, wg_ref[:, cols], preferred_element_type=F32)
        up = jnp.dot(h, wu_ref[:, cols], preferred_element_type=F32)
        act = (gate * jax.nn.sigmoid(gate) * up * 0.5).astype(BF16)
        acc = acc + jnp.dot(act, wd_ref[cols, :], preferred_element_type=F32)
    return acc


def _ffn_kernel(x_ref, gain_ref, wg_ref, wu_ref, wd_ref, o_ref, *, ff_tile):
    o_ref[...] = _swiglu_residual(x_ref[...], gain_ref, wg_ref, wu_ref, wd_ref, ff_tile)


def _proj_ffn_kernel(x_ref, osb_ref, orw_ref, wo_ref, gain_ref, wg_ref, wu_ref, wd_ref,
                     o_ref, *, ff_tile):
    mixed = (jnp.dot(osb_ref[...], wo_ref[:SB_WIDTH, :], preferred_element_type=F32)
             + jnp.dot(orw_ref[...], wo_ref[SB_WIDTH:, :], preferred_element_type=F32))
    x = mixed + x_ref[...]
    o_ref[...] = _swiglu_residual(x, gain_ref, wg_ref, wu_ref, wd_ref, ff_tile)


def _ffn_tiles(n_tokens, d_ff):
    row_tile = 512 if n_tokens % 512 == 0 else n_tokens
    ff_tile = d_ff // 2 if (d_ff // 2) % LANES == 0 else d_ff
    return row_tile, ff_tile


def _ffn_call(x, gain, wg, wu, wd, proj=None):
    n, d = x.shape
    d_ff = wg.shape[1]
    row_tile, ff_tile = _ffn_tiles(n, d_ff)
    row_spec = pl.BlockSpec((row_tile, d), lambda i: (i, 0))
    weight_specs = [_const_spec((1, d)), _const_spec((d, d_ff)), _const_spec((d, d_ff)),
                    _const_spec((d_ff, d))]
    if proj is None:
        body = functools.partial(_ffn_kernel, ff_tile=ff_tile)
        in_specs = [row_spec] + weight_specs
        args = (x, gain, wg, wu, wd)
        name = "ffn1"
    else:
        o_sb, o_rw, w_out = proj
        body = functools.partial(_proj_ffn_kernel, ff_tile=ff_tile)
        in_specs = [row_spec,
                    pl.BlockSpec((row_tile, SB_WIDTH), lambda i: (i, 0)),
                    pl.BlockSpec((row_tile, RW_WIDTH), lambda i: (i, 0)),
                    _const_spec(w_out.shape)] + weight_specs
        args = (x, o_sb, o_rw, w_out, gain, wg, wu, wd)
        name = "ffn2"
    return pl.pallas_call(
        body,
        out_shape=jax.ShapeDtypeStruct((n, d), F32),
        grid=(n // row_tile,),
        in_specs=in_specs,
        out_specs=row_spec,
        compiler_params=pltpu.CompilerParams(
            dimension_semantics=("parallel",), vmem_limit_bytes=VMEM_LIMIT),
        name=name,
    )(*args)


def _mix_in_kernel(x_ref, gain_ref, w_ref, wvt_ref, qg_ref, kg_ref, mean_bd_ref,
                   q_ref, k_ref, vt_ref, rw_ref):
    h = _rms_norm_rows(x_ref[...], gain_ref[...]).astype(BF16)
    mean_bd = mean_bd_ref[...]

    def head_norm(t, gain):
        ms = _head_reduce(t * t, mean_bd, two_pass=False)
        return t * lax.rsqrt(ms + RMS_EPS) * gain

    q = jnp.dot(h, w_ref[:, 0:SB_WIDTH], preferred_element_type=F32)
    q_ref[...] = (head_norm(q, qg_ref[...]) * (HEAD_DIM ** -0.5)).astype(BF16)
    k = jnp.dot(h, w_ref[:, SB_WIDTH:2 * SB_WIDTH], preferred_element_type=F32)
    k_ref[...] = head_norm(k, kg_ref[...]).astype(BF16)
    vt = _dot_nt(wvt_ref[...], h).astype(BF16)
    for t in range(vt_ref.shape[0]):
        vt_ref[t] = vt[:, t * SB_TILE:(t + 1) * SB_TILE]
    rw_ref[...] = jnp.dot(h, w_ref[:, 3 * SB_WIDTH:], preferred_element_type=F32)


def _mix_in_call(x, gain, w_in, wv_t, q_gain, k_gain, mean_bd):
    n, d = x.shape
    row_tile = 2 * SB_TILE
    assert n % row_tile == 0
    row = lambda width: pl.BlockSpec((row_tile, width), lambda i: (i, 0))
    return pl.pallas_call(
        _mix_in_kernel,
        out_shape=(jax.ShapeDtypeStruct((n, SB_WIDTH), BF16),
                   jax.ShapeDtypeStruct((n, SB_WIDTH), BF16),
                   jax.ShapeDtypeStruct((n // SB_TILE, SB_WIDTH, SB_TILE), BF16),
                   jax.ShapeDtypeStruct((n, RW_PROJ), F32)),
        grid=(n // row_tile,),
        in_specs=[row(d), _const_spec((1, d)), _const_spec(w_in.shape),
                  _const_spec(wv_t.shape), _const_spec((1, SB_WIDTH)),
                  _const_spec((1, SB_WIDTH)), _const_spec(mean_bd.shape)],
        out_specs=(row(SB_WIDTH), row(SB_WIDTH),
                   pl.BlockSpec((row_tile // SB_TILE, SB_WIDTH, SB_TILE), lambda i: (i, 0, 0)),
                   row(RW_PROJ)),
        compiler_params=pltpu.CompilerParams(
            dimension_semantics=("parallel",), vmem_limit_bytes=VMEM_LIMIT),
        name="mix_in",
    )(x, gain, w_in, wv_t, q_gain, k_gain, mean_bd)


def _sb_attn_kernel(q_ref, k_ref, vt_ref, gain_ref, o_ref, later_ref, acc_ref, z_ref):
    tile = pl.program_id(2)
    q = q_ref[...]
    lane = lax.broadcasted_iota(jnp.int32, q.shape, 1)
    first_head = lane < HEAD_DIM
    zero = jnp.zeros_like(q)
    q_heads = jnp.concatenate([jnp.where(first_head, q, zero), jnp.where(first_head, zero, q)],
                              axis=0)
    key_pos = lax.broadcasted_iota(jnp.int32, (SB_TILE, 2 * SB_TILE), 0)
    col = lax.broadcasted_iota(jnp.int32, (SB_TILE, 2 * SB_TILE), 1)
    causal = key_pos < jnp.where(col >= SB_TILE, col - SB_TILE, col)
    kk = lax.broadcasted_iota(jnp.int32, (SB_TILE, SB_TILE), 0)
    jj = lax.broadcasted_iota(jnp.int32, (SB_TILE, SB_TILE), 1)
    later_keys = jnp.where(jj > kk, 1.0, 0.0).astype(BF16)

    def scores(j):
        start = pl.multiple_of(jnp.maximum(j, 0) * SB_TILE, SB_TILE)
        return lax.dot_general(k_ref[pl.ds(start, SB_TILE), :], q_heads,
                               (((1,), (1,)), ((), ())), preferred_element_type=F32)

    def log_betas(z, diagonal):
        neg_part = jnp.minimum(z, 0.0)
        neg_relu = neg_part - z
        soft = jnp.log(1.0 + jnp.exp(neg_part + neg_relu))
        log_1m_beta = neg_relu - soft
        if diagonal:
            log_1m_beta = jnp.where(causal, log_1m_beta, 0.0)
        return neg_part - soft, log_1m_beta

    def key_tiles(tiles, logits, diagonal):
        parts = [log_betas(z, diagonal) for z in logits]
        later = later_ref[...]
        acc0 = acc_ref[:HEAD_DIM, :]
        acc1 = acc_ref[HEAD_DIM:, :]
        for j, (log_beta, log_1m_beta) in zip(tiles, parts):
            v_t = vt_ref[j]
            sums = jnp.dot(later_keys, log_1m_beta.astype(BF16), preferred_element_type=F32)
            att = jnp.exp(log_beta + sums + later)
            if diagonal:
                att = jnp.where(causal, att, 0.0)
            att = att.astype(BF16)
            acc0 = acc0 + jnp.dot(v_t[:HEAD_DIM, :], att[:, :SB_TILE],
                                  preferred_element_type=F32)
            acc1 = acc1 + jnp.dot(v_t[HEAD_DIM:, :], att[:, SB_TILE:],
                                  preferred_element_type=F32)
            later = later + jnp.sum(log_1m_beta, axis=0, keepdims=True)
        later_ref[...] = later
        acc_ref[:HEAD_DIM, :] = acc0
        acc_ref[HEAD_DIM:, :] = acc1

    later_ref[...] = jnp.zeros_like(later_ref)
    acc_ref[...] = jnp.zeros_like(acc_ref)
    n_pairs = tile // 2
    z_diag = scores(tile)
    z_ref[0, 0] = scores(tile - 1)
    z_ref[0, 1] = scores(tile - 2)
    key_tiles([tile], [z_diag], diagonal=True)

    def tile_pair(t, _):
        slot = t % 2
        j = tile - 1 - 2 * t
        logits = [z_ref[slot, 0], z_ref[slot, 1]]
        z_ref[1 - slot, 0] = scores(j - 2)
        z_ref[1 - slot, 1] = scores(j - 3)
        key_tiles([j, j - 1], logits, diagonal=False)
        return 0

    lax.fori_loop(0, n_pairs, tile_pair, 0)

    @pl.when(tile % 2 == 1)
    def _():
        key_tiles([0], [z_ref[n_pairs % 2, 0]], diagonal=False)

    def head_norm(acc):
        ms = jnp.sum(acc * acc, axis=0, keepdims=True) * (1.0 / HEAD_DIM)
        return acc * lax.rsqrt(ms + RMS_EPS)

    out_t = jnp.concatenate([head_norm(acc_ref[:HEAD_DIM, :]), head_norm(acc_ref[HEAD_DIM:, :])],
                            axis=0)
    o_ref[...] = (out_t.T * gain_ref[...]).astype(o_ref.dtype)


def _sb_attn_call(q, k, v_t, out_gain):
    b, s, width = q.shape
    pairs = width // LANES
    tiles = s // SB_TILE
    blk_spec = pl.BlockSpec((None, SB_TILE, LANES), lambda bi, p, i: (bi, i, p))
    return pl.pallas_call(
        _sb_attn_kernel,
        out_shape=jax.ShapeDtypeStruct((b, s, width), BF16),
        grid=(b, pairs, tiles),
        in_specs=[blk_spec,
                  pl.BlockSpec((None, s, LANES), lambda bi, p, i: (bi, 0, p)),
                  pl.BlockSpec((tiles, LANES, SB_TILE), lambda bi, p, i: (bi, p, 0)),
                  pl.BlockSpec((1, LANES), lambda bi, p, i: (0, p))],
        out_specs=blk_spec,
        scratch_shapes=[pltpu.VMEM((1, 2 * SB_TILE), F32),
                        pltpu.VMEM((LANES, SB_TILE), F32),
                        pltpu.VMEM((2, 2, SB_TILE, 2 * SB_TILE), F32)],
        compiler_params=pltpu.CompilerParams(
            dimension_semantics=("parallel", "parallel", "arbitrary"),
            vmem_limit_bytes=VMEM_LIMIT),
        name="sb_attn",
    )(q, k, v_t, out_gain)


def _rwkv_kernel(u_ref, mu_ref, w0_ref, w2_ref, a0_ref, a2_ref, g2_ref, kk_ref, ka_ref,
                 rk_ref, lnw_ref, lnb_ref, ones_bd_ref, tri_ref, o_ref,
                 prev_ref, state_ref, r_s, kf_s, v_s, kn_s, eta_s, ld_s, y_s):
    n_rows = u_ref.shape[0]

    @pl.when(pl.program_id(1) == 0)
    def _():
        prev_ref[...] = jnp.zeros_like(prev_ref)
        state_ref[...] = jnp.zeros_like(state_ref)

    u = u_ref[...]
    row_id = lax.broadcasted_iota(jnp.int32, u.shape, 0)
    shifted = jnp.where(row_id == 0, prev_ref[...], pltpu.roll(u, 1, axis=0))
    prev_ref[...] = u[n_rows - 1:n_rows, :]
    u = u + (shifted - u) * mu_ref[...]

    x_r = u[:, 0:RW_WIDTH]
    x_k = u[:, RW_WIDTH:2 * RW_WIDTH]
    x_v = u[:, 2 * RW_WIDTH:3 * RW_WIDTH]
    x_wa = u[:, 3 * RW_WIDTH:3 * RW_WIDTH + DECAY_LORA + AAA_LORA]
    x_g = u[:, 3 * RW_WIDTH + DECAY_LORA + AAA_LORA:]
    ones_bd = ones_bd_ref[...]

    pre = -(w0_ref[...] + _dot_f32(jnp.tanh(x_wa), w2_ref[...]))
    log_w = -(jnp.maximum(pre, 0.0) + jnp.log(1.0 + jnp.exp(-jnp.abs(pre)))) - 0.5
    ld_s[...] = -jnp.exp(log_w)
    eta = jax.nn.sigmoid(a0_ref[...] + _dot(x_wa, a2_ref[...]))
    gate = _dot(jax.nn.sigmoid(x_g), g2_ref[...])
    kk = x_k * kk_ref[...]
    kk_norm = jnp.sqrt(_head_reduce(kk * kk, ones_bd, two_pass=True))
    kf = x_k * (1.0 + (eta - 1.0) * ka_ref[...])
    bonus = _head_reduce(x_r * kf * rk_ref[...], ones_bd, two_pass=True) * x_v
    r_s[...] = x_r
    kf_s[...] = kf
    v_s[...] = x_v
    kn_s[...] = kk / jnp.maximum(kk_norm, KK_NORM_FLOOR)
    eta_s[...] = eta

    lane = lax.broadcasted_iota(jnp.int32, (RW_CHUNK, LANES), 1)
    first_head = lane < HEAD_DIM
    ri = lax.broadcasted_iota(jnp.int32, (LANES, LANES), 0)
    ci = lax.broadcasted_iota(jnp.int32, (LANES, LANES), 1)
    same_head = (ri < HEAD_DIM) == (ci < HEAD_DIM)
    strict_lower = same_head & (ci < ri)
    lower = same_head & (ci <= ri)
    eye = (ri == ci).astype(F32)
    tri = tri_ref[...]

    def stack_heads(t):
        return jnp.concatenate([jnp.where(first_head, t, 0.0), jnp.where(first_head, 0.0, t)],
                               axis=0)

    n_pairs = RW_WIDTH // LANES

    def chunk_operands(rows):
        ld = ld_s[rows, :]
        cum = _prefix_sum(tri, ld)
        total = cum[RW_CHUNK - 1:RW_CHUNK, :]
        e_incl = jnp.exp(cum)
        e_inv = jnp.exp(-cum)
        e_tail = jnp.exp(total - cum)
        w_total = jnp.exp(total)
        kn = kn_s[rows, :]
        kf_c = kf_s[rows, :]
        b = kn * eta_s[rows, :]
        a_dec = -kn * jnp.exp(cum - ld)
        r_dec = r_s[rows, :] * e_incl
        b_inv = b * e_inv
        k_inv = kf_c * e_inv
        b_tail = b * e_tail
        k_tail = kf_c * e_tail
        v_c = v_s[rows, :]
        units = []
        for p in range(n_pairs):
            cols = slice(p * LANES, (p + 1) * LANES)
            units.append(dict(
                a_sm=stack_heads(a_dec[:, cols]), r_sm=stack_heads(r_dec[:, cols]),
                v_sm=stack_heads(v_c[:, cols]), b_inv=b_inv[:, cols], k_inv=k_inv[:, cols],
                b_tail_t=stack_heads(b_tail[:, cols]).T, k_tail_t=stack_heads(k_tail[:, cols]).T,
                decay_diag=eye * w_total[:, cols]))
        return units

    def chunk_group(g, _):
        row_slices = [pl.ds(pl.multiple_of((g * RW_GROUP + c) * RW_CHUNK, RW_CHUNK), RW_CHUNK)
                      for c in range(RW_GROUP)]
        units = [u for rows in row_slices for u in chunk_operands(rows)]
        scores = [_dot_nt(jnp.concatenate([u["a_sm"], u["r_sm"]], axis=0),
                          jnp.concatenate([u["b_inv"], u["b_inv"], u["k_inv"], u["k_inv"]],
                                          axis=0)) for u in units]
        l_ab = [jnp.where(strict_lower, s[:LANES, :LANES], 0.0) for s in scores]
        l_ak = [jnp.where(strict_lower, s[:LANES, LANES:], 0.0) for s in scores]
        m_rb = [jnp.where(lower, s[LANES:, :LANES], 0.0) for s in scores]
        m_rk = [jnp.where(lower, s[LANES:, LANES:], 0.0) for s in scores]
        lakv = [_dot(l, u["v_sm"]) for l, u in zip(l_ak, units)]
        inv = [eye + l for l in l_ab]
        power = l_ab
        for _ in range(5):
            power = [_dot(pw, pw) for pw in power]
            inv = [iv + _dot(iv, pw) for iv, pw in zip(inv, power)]
        pu = [_dot(iv, jnp.concatenate([u["a_sm"], lv], axis=1))
              for iv, u, lv in zip(inv, units, lakv)]
        qy = [_dot(m, x) for m, x in zip(m_rb, pu)]
        y0 = [_dot(m, u["v_sm"]) for m, u in zip(m_rk, units)]
        gh = [_dot(u["b_tail_t"], x) for u, x in zip(units, pu)]
        kv = [_dot(u["k_tail_t"], u["v_sm"]) for u in units]
        states = [state_ref[p] for p in range(n_pairs)]
        for c, rows in enumerate(row_slices):
            new_states = []
            for p in range(n_pairs):
                i = c * n_pairs + p
                q_sm = units[i]["r_sm"] + qy[i][:, :LANES]
                y_sm = _dot(q_sm, states[p]) + (qy[i][:, LANES:] + y0[i])
                g_bd = gh[i][:, :LANES] + units[i]["decay_diag"]
                new_states.append(_dot(g_bd, states[p]) + (gh[i][:, LANES:] + kv[i]))
                y_s[rows, p * LANES:(p + 1) * LANES] = y_sm[:RW_CHUNK, :] + y_sm[RW_CHUNK:, :]
            states = new_states
        for p in range(n_pairs):
            state_ref[p] = states[p]
        return 0

    lax.fori_loop(0, n_rows // (RW_CHUNK * RW_GROUP), chunk_group, 0)

    y = y_s[...]
    centered = y - _head_reduce(y, ones_bd, two_pass=True) * (1.0 / HEAD_DIM)
    var = _head_reduce(centered * centered, ones_bd, two_pass=True) * (1.0 / HEAD_DIM)
    yn = centered * lax.rsqrt(var + LNX_EPS) * lnw_ref[...] + lnb_ref[...]
    o_ref[...] = ((yn + bonus) * gate).astype(o_ref.dtype)


def _rwkv_call(rw_in, mu, w0, w2_pad, a0, a2_pad, g2, k_k, k_a, r_k, ln_w, ln_b, ones_bd, tri):
    b, s, width = rw_in.shape
    row_tile = 256 if s % 256 == 0 else s
    vec = lambda n: _const_spec((1, n))
    scratch_rows = pltpu.VMEM((row_tile, RW_WIDTH), F32)
    return pl.pallas_call(
        _rwkv_kernel,
        out_shape=jax.ShapeDtypeStruct((b, s, RW_WIDTH), BF16),
        grid=(b, s // row_tile),
        in_specs=[pl.BlockSpec((None, row_tile, width), lambda bi, j: (bi, j, 0)),
                  vec(width), vec(RW_WIDTH), _const_spec(w2_pad.shape), vec(RW_WIDTH),
                  _const_spec(a2_pad.shape), _const_spec(g2.shape), vec(RW_WIDTH),
                  vec(RW_WIDTH), vec(RW_WIDTH), vec(RW_WIDTH), vec(RW_WIDTH),
                  _const_spec(ones_bd.shape), _const_spec(tri.shape)],
        out_specs=pl.BlockSpec((None, row_tile, RW_WIDTH), lambda bi, j: (bi, j, 0)),
        scratch_shapes=[pltpu.VMEM((1, width), F32),
                        pltpu.VMEM((RW_WIDTH // LANES, LANES, LANES), F32)]
                       + [scratch_rows] * 7,
        compiler_params=pltpu.CompilerParams(
            dimension_semantics=("parallel", "arbitrary"), vmem_limit_bytes=VMEM_LIMIT),
        name="rwkv7",
    )(rw_in, mu, w0, w2_pad, a0, a2_pad, g2, k_k, k_a, r_k, ln_w, ln_b, ones_bd, tri)


def _head_block_diag(value, dtype):
    idx = jnp.arange(MXU_DIM) // HEAD_DIM
    return jnp.where(idx[:, None] == idx[None, :], value, 0.0).astype(dtype)


def kernel(x, norm_ffn1, ffn1_gate, ffn1_up, ffn1_down, norm_mix, w_in, sb_q_norm, sb_k_norm,
           sb_out_norm, rw_mu, rw_w0, rw_w2, rw_a0, rw_a2, rw_g2, rw_k_k, rw_k_a, rw_r_k,
           rw_ln_w, rw_ln_b, w_out, norm_ffn2, ffn2_gate, ffn2_up, ffn2_down):
    b, s, d = x.shape
    n = b * s
    heads = SB_WIDTH // HEAD_DIM
    mean_bd = _head_block_diag(1.0 / HEAD_DIM, BF16)
    ones_bd = _head_block_diag(1.0, BF16)
    tri = jnp.tril(jnp.ones((RW_CHUNK, RW_CHUNK), BF16))
    row = lambda t: t.reshape(1, -1)

    h = x.reshape(n, d)
    for l in range(norm_ffn1.shape[0]):
        h = _ffn_call(h, row(norm_ffn1[l]), ffn1_gate[l].astype(BF16), ffn1_up[l].astype(BF16),
                      ffn1_down[l].astype(BF16))

        w_in_l = w_in[l].astype(BF16)
        q, k, v_t, rw_in = _mix_in_call(
            h, row(norm_mix[l]), w_in_l, w_in_l[:, 2 * SB_WIDTH:3 * SB_WIDTH].T,
            row(jnp.tile(sb_q_norm[l], heads)), row(jnp.tile(sb_k_norm[l], heads)), mean_bd)

        o_sb = _sb_attn_call(q.reshape(b, s, SB_WIDTH), k.reshape(b, s, SB_WIDTH), v_t,
                             row(sb_out_norm[l]))

        zeros = jnp.zeros((DECAY_LORA, RW_WIDTH), F32)
        w2_pad = jnp.concatenate([rw_w2[l], zeros], axis=0)
        a2_pad = jnp.concatenate([zeros, rw_a2[l]], axis=0).astype(BF16)
        o_rw = _rwkv_call(rw_in.reshape(b, s, RW_PROJ), row(rw_mu[l]), row(rw_w0[l]), w2_pad,
                          row(rw_a0[l]), a2_pad, rw_g2[l].astype(BF16), row(rw_k_k[l]),
                          row(rw_k_a[l]), row(rw_r_k[l]), row(rw_ln_w[l]), row(rw_ln_b[l]),
                          ones_bd, tri)

        h = _ffn_call(h, row(norm_ffn2[l]), ffn2_gate[l].astype(BF16), ffn2_up[l].astype(BF16),
                      ffn2_down[l].astype(BF16),
                      proj=(o_sb.reshape(n, SB_WIDTH), o_rw.reshape(n, RW_WIDTH),
                            w_out[l].astype(BF16)))
    return h.reshape(b, s, d)
```

```python
import functools

import jax
import jax.numpy as jnp
from jax import lax
from jax.experimental import pallas as pl
from jax.experimental.pallas import tpu as pltpu

F32 = jnp.float32
BF16 = jnp.bfloat16

HEAD_DIM = 64
SB_WIDTH = 512
RW_WIDTH = 512
DECAY_LORA = 64
AAA_LORA = 64
GATE_LORA = 128
RW_PROJ = 3 * RW_WIDTH + DECAY_LORA + AAA_LORA + GATE_LORA
RMS_EPS = 1e-6
LNX_EPS = 64e-5
KK_NORM_FLOOR = 1e-12

LANES = 128
MXU_DIM = 256
RW_CHUNK = 64
RW_GROUP = 2
SB_TILE = 256
SB_QTILE = 2 * SB_TILE
MASKED_LOG_BETA = -1e30
VMEM_LIMIT = 56 * 1024 * 1024


def _const_spec(shape):
    zeros = (0,) * len(shape)
    return pl.BlockSpec(shape, lambda *_: zeros, pipeline_mode=pl.Buffered(1))


def _dot(a, b):
    return jnp.dot(a.astype(BF16), b.astype(BF16), preferred_element_type=F32)


def _dot_nt(a, b):
    return lax.dot_general(a.astype(BF16), b.astype(BF16), (((1,), (1,)), ((), ())),
                           preferred_element_type=F32)


def _dot_f32(a, b):
    return jnp.dot(a, b, precision=lax.Precision.HIGHEST, preferred_element_type=F32)


def _prefix_sum(tri, x):
    hi = x.astype(BF16)
    rest = x - hi.astype(F32)
    mid = rest.astype(BF16)
    lo = (rest - mid.astype(F32)).astype(BF16)
    return (jnp.dot(tri, hi, preferred_element_type=F32)
            + jnp.dot(tri, mid, preferred_element_type=F32)
            + jnp.dot(tri, lo, preferred_element_type=F32))


def _rms_norm_rows(x, gain):
    ms = jnp.mean(x * x, axis=-1, keepdims=True)
    return x * lax.rsqrt(ms + RMS_EPS) * gain


def _head_reduce(t, bd, two_pass):
    outs = []
    for half in range(t.shape[1] // MXU_DIM):
        th = t[:, half * MXU_DIM:(half + 1) * MXU_DIM]
        hi = th.astype(BF16)
        red = jnp.dot(hi, bd, preferred_element_type=F32)
        if two_pass:
            lo = (th - hi.astype(F32)).astype(BF16)
            red = red + jnp.dot(lo, bd, preferred_element_type=F32)
        outs.append(red)
    return jnp.concatenate(outs, axis=1)


def _swiglu_residual(x, gain_ref, wg_ref, wu_ref, wd_ref, ff_tile):
    h = _rms_norm_rows(x, gain_ref[...]).astype(BF16)
    acc = x
    for c in range(wg_ref.shape[1] // ff_tile):
        cols = slice(c * ff_tile, (c + 1) * ff_tile)
        gate = jnp.dot(h, wg_ref[:, cols], preferred_element_type=F32)
        up = jnp.dot(h, wu_ref[:, cols], preferred_element_type=F32)
        act = (gate * jax.nn.sigmoid(gate) * up * 0.5).astype(BF16)
        acc = acc + jnp.dot(act, wd_ref[cols, :], preferred_element_type=F32)
    return acc


def _ffn_kernel(x_ref, gain_ref, wg_ref, wu_ref, wd_ref, o_ref, *, ff_tile):
    o_ref[...] = _swiglu_residual(x_ref[...], gain_ref, wg_ref, wu_ref, wd_ref, ff_tile)


def _proj_ffn_kernel(x_ref, osb_ref, orw_ref, wo_ref, gain_ref, wg_ref, wu_ref, wd_ref,
                     o_ref, *, ff_tile):
    mixed = (jnp.dot(osb_ref[...], wo_ref[:SB_WIDTH, :], preferred_element_type=F32)
             + jnp.dot(orw_ref[...], wo_ref[SB_WIDTH:, :], preferred_element_type=F32))
    x = mixed + x_ref[...]
    o_ref[...] = _swiglu_residual(x, gain_ref, wg_ref, wu_ref, wd_ref, ff_tile)


def _ffn_tiles(n_tokens, d_ff):
    row_tile = 512 if n_tokens % 512 == 0 else n_tokens
    ff_tile = d_ff // 2 if (d_ff // 2) % LANES == 0 else d_ff
    return row_tile, ff_tile


def _ffn_call(x, gain, wg, wu, wd, proj=None):
    n, d = x.shape
    d_ff = wg.shape[1]
    row_tile, ff_tile = _ffn_tiles(n, d_ff)
    row_spec = pl.BlockSpec((row_tile, d), lambda i: (i, 0))
    weight_specs = [_const_spec((1, d)), _const_spec((d, d_ff)), _const_spec((d, d_ff)),
                    _const_spec((d_ff, d))]
    if proj is None:
        body = functools.partial(_ffn_kernel, ff_tile=ff_tile)
        in_specs = [row_spec] + weight_specs
        args = (x, gain, wg, wu, wd)
        name = "ffn1"
    else:
        o_sb, o_rw, w_out = proj
        body = functools.partial(_proj_ffn_kernel, ff_tile=ff_tile)
        in_specs = [row_spec,
                    pl.BlockSpec((row_tile, SB_WIDTH), lambda i: (i, 0)),
                    pl.BlockSpec((row_tile, RW_WIDTH), lambda i: (i, 0)),
                    _const_spec(w_out.shape)] + weight_specs
        args = (x, o_sb, o_rw, w_out, gain, wg, wu, wd)
        name = "ffn2"
    return pl.pallas_call(
        body,
        out_shape=jax.ShapeDtypeStruct((n, d), F32),
        grid=(n // row_tile,),
        in_specs=in_specs,
        out_specs=row_spec,
        compiler_params=pltpu.CompilerParams(
            dimension_semantics=("parallel",), vmem_limit_bytes=VMEM_LIMIT),
        name=name,
    )(*args)


def _mix_in_kernel(x_ref, gain_ref, w_ref, wvt_ref, qg_ref, kg_ref, mean_bd_ref,
                   q_ref, k_ref, vt_ref, rw_ref):
    h = _rms_norm_rows(x_ref[...], gain_ref[...]).astype(BF16)
    mean_bd = mean_bd_ref[...]

    def head_norm(t, gain):
        ms = _head_reduce(t * t, mean_bd, two_pass=False)
        return t * lax.rsqrt(ms + RMS_EPS) * gain

    q = jnp.dot(h, w_ref[:, 0:SB_WIDTH], preferred_element_type=F32)
    q_ref[...] = (head_norm(q, qg_ref[...]) * (HEAD_DIM ** -0.5)).astype(BF16)
    k = jnp.dot(h, w_ref[:, SB_WIDTH:2 * SB_WIDTH], preferred_element_type=F32)
    k_ref[...] = head_norm(k, kg_ref[...]).astype(BF16)
    vt = _dot_nt(wvt_ref[...], h).astype(BF16)
    for t in range(vt_ref.shape[0]):
        vt_ref[t] = vt[:, t * SB_TILE:(t + 1) * SB_TILE]
    rw_ref[...] = jnp.dot(h, w_ref[:, 3 * SB_WIDTH:], preferred_element_type=F32)


def _mix_in_call(x, gain, w_in, wv_t, q_gain, k_gain, mean_bd):
    n, d = x.shape
    row_tile = 2 * SB_TILE
    assert n % row_tile == 0
    row = lambda width: pl.BlockSpec((row_tile, width), lambda i: (i, 0))
    return pl.pallas_call(
        _mix_in_kernel,
        out_shape=(jax.ShapeDtypeStruct((n, SB_WIDTH), BF16),
                   jax.ShapeDtypeStruct((n, SB_WIDTH), BF16),
                   jax.ShapeDtypeStruct((n // SB_TILE, SB_WIDTH, SB_TILE), BF16),
                   jax.ShapeDtypeStruct((n, RW_PROJ), F32)),
        grid=(n // row_tile,),
        in_specs=[row(d), _const_spec((1, d)), _const_spec(w_in.shape),
                  _const_spec(wv_t.shape), _const_spec((1, SB_WIDTH)),
                  _const_spec((1, SB_WIDTH)), _const_spec(mean_bd.shape)],
        out_specs=(row(SB_WIDTH), row(SB_WIDTH),
                   pl.BlockSpec((row_tile // SB_TILE, SB_WIDTH, SB_TILE), lambda i: (i, 0, 0)),
                   row(RW_PROJ)),
        compiler_params=pltpu.CompilerParams(
            dimension_semantics=("parallel",), vmem_limit_bytes=VMEM_LIMIT),
        name="mix_in",
    )(x, gain, w_in, wv_t, q_gain, k_gain, mean_bd)


def _sb_attn_kernel(q_ref, k_ref, vt_ref, gain_ref, o_ref,
                    later_ref, acc_ref, z_ref, lb_ref, l1m_ref, row0_ref):
    tile = pl.program_id(2)
    n_tiles = 2 * tile + 2
    q_t = q_ref[...].astype(F32).T
    first_head = lax.broadcasted_iota(jnp.int32, q_t.shape, 0) < HEAD_DIM
    q_heads = jnp.concatenate([jnp.where(first_head, q_t, 0.0), jnp.where(first_head, 0.0, q_t)],
                              axis=1).astype(BF16)
    kk = lax.broadcasted_iota(jnp.int32, (SB_TILE, SB_TILE), 0)
    jj = lax.broadcasted_iota(jnp.int32, (SB_TILE, SB_TILE), 1)
    later_keys = jnp.where(jj > kk, 1.0, 0.0).astype(BF16)

    def causal_mask(key_offset):
        key_pos = lax.broadcasted_iota(jnp.int32, (SB_TILE, 2 * SB_QTILE), 0) + key_offset
        col = lax.broadcasted_iota(jnp.int32, (SB_TILE, 2 * SB_QTILE), 1)
        return key_pos < jnp.where(col >= SB_QTILE, col - SB_QTILE, col)

    def key_tile(t):
        return jnp.maximum(n_tiles - 1 - t, 0)

    def scores(t):
        start = pl.multiple_of(key_tile(t) * SB_TILE, SB_TILE)
        return jnp.dot(k_ref[pl.ds(start, SB_TILE), :], q_heads, preferred_element_type=F32)

    def stage1(z, mask=None):
        neg_part = jnp.minimum(z, 0.0)
        neg_relu = neg_part - z
        soft = jnp.log(1.0 + jnp.exp(neg_part + neg_relu))
        log_beta = neg_part - soft
        log_1m_beta = neg_relu - soft
        if mask is not None:
            log_beta = jnp.where(mask, log_beta, MASKED_LOG_BETA)
            log_1m_beta = jnp.where(mask, log_1m_beta, 0.0)
        return log_beta, log_1m_beta.astype(BF16), log_1m_beta[0:1, :]

    def stage2(t, slot, sums):
        v_t = vt_ref[key_tile(t)]
        later = later_ref[...]
        att = jnp.exp(lb_ref[slot] + sums + later).astype(BF16)
        for h in range(2):
            rows = slice(h * HEAD_DIM, (h + 1) * HEAD_DIM)
            acc_ref[rows, :] += jnp.dot(v_t[rows, :], att[:, h * SB_QTILE:(h + 1) * SB_QTILE],
                                        preferred_element_type=F32)
        later_ref[...] = later + sums[0:1, :] + row0_ref[slot]

    def trip(t, cur, mask=None):
        other = 1 - cur
        z_next = scores(t + 1)
        sums = jnp.dot(later_keys, l1m_ref[other], preferred_element_type=F32)
        log_beta, log_1m_beta, key0_row = stage1(z_ref[cur], mask)
        stage2(t - 1, other, sums)
        lb_ref[cur] = log_beta
        l1m_ref[cur] = log_1m_beta
        row0_ref[cur] = key0_row
        z_ref[other] = z_next

    later_ref[...] = jnp.zeros_like(later_ref)
    acc_ref[...] = jnp.zeros_like(acc_ref)
    lb_ref[0], l1m_ref[0], row0_ref[0] = stage1(scores(0), causal_mask(SB_TILE))
    z_ref[1] = scores(1)
    trip(1, 1, causal_mask(0))

    def trip_pair(i, _):
        trip(2 * i + 2, 0)
        trip(2 * i + 3, 1)
        return 0

    lax.fori_loop(0, tile, trip_pair, 0)
    stage2(n_tiles - 1, 1, jnp.dot(later_keys, l1m_ref[1], preferred_element_type=F32))

    def head_norm(acc):
        ms = jnp.sum(acc * acc, axis=0, keepdims=True) * (1.0 / HEAD_DIM)
        return acc * lax.rsqrt(ms + RMS_EPS)

    out_t = jnp.concatenate([head_norm(acc_ref[:HEAD_DIM, :]), head_norm(acc_ref[HEAD_DIM:, :])],
                            axis=0)
    o_ref[...] = (out_t.T * gain_ref[...]).astype(o_ref.dtype)


def _sb_attn_call(q, k, v_t, out_gain):
    b, s, width = q.shape
    pairs = width // LANES
    assert s % SB_QTILE == 0
    blk_spec = pl.BlockSpec((None, SB_QTILE, LANES), lambda bi, p, i: (bi, i, p))
    both_heads = 2 * SB_QTILE
    return pl.pallas_call(
        _sb_attn_kernel,
        out_shape=jax.ShapeDtypeStruct((b, s, width), BF16),
        grid=(b, pairs, s // SB_QTILE),
        in_specs=[blk_spec,
                  pl.BlockSpec((None, s, LANES), lambda bi, p, i: (bi, 0, p)),
                  pl.BlockSpec((s // SB_TILE, LANES, SB_TILE), lambda bi, p, i: (bi, p, 0)),
                  pl.BlockSpec((1, LANES), lambda bi, p, i: (0, p))],
        out_specs=blk_spec,
        scratch_shapes=[pltpu.VMEM((1, both_heads), F32),
                        pltpu.VMEM((LANES, SB_QTILE), F32),
                        pltpu.VMEM((2, SB_TILE, both_heads), F32),
                        pltpu.VMEM((2, SB_TILE, both_heads), F32),
                        pltpu.VMEM((2, SB_TILE, both_heads), BF16),
                        pltpu.VMEM((2, 1, both_heads), F32)],
        compiler_params=pltpu.CompilerParams(
            dimension_semantics=("parallel", "parallel", "arbitrary"),
            vmem_limit_bytes=VMEM_LIMIT),
        name="sb_attn",
    )(q, k, v_t, out_gain)


def _rwkv_kernel(u_ref, mu_ref, w0_ref, w2_ref, a0_ref, a2_ref, g2_ref, kk_ref, ka_ref,
                 rk_ref, lnw_ref, lnb_ref, ones_bd_ref, tri_ref, o_ref,
                 prev_ref, state_ref, r_s, kf_s, v_s, kn_s, eta_s, ld_s, y_s):
    n_rows = u_ref.shape[0]

    @pl.when(pl.program_id(1) == 0)
    def _():
        prev_ref[...] = jnp.zeros_like(prev_ref)
        state_ref[...] = jnp.zeros_like(state_ref)

    u = u_ref[...]
    row_id = lax.broadcasted_iota(jnp.int32, u.shape, 0)
    shifted = jnp.where(row_id == 0, prev_ref[...], pltpu.roll(u, 1, axis=0))
    prev_ref[...] = u[n_rows - 1:n_rows, :]
    u = u + (shifted - u) * mu_ref[...]

    x_r = u[:, 0:RW_WIDTH]
    x_k = u[:, RW_WIDTH:2 * RW_WIDTH]
    x_v = u[:, 2 * RW_WIDTH:3 * RW_WIDTH]
    x_wa = u[:, 3 * RW_WIDTH:3 * RW_WIDTH + DECAY_LORA + AAA_LORA]
    x_g = u[:, 3 * RW_WIDTH + DECAY_LORA + AAA_LORA:]
    ones_bd = ones_bd_ref[...]

    pre = -(w0_ref[...] + _dot_f32(jnp.tanh(x_wa), w2_ref[...]))
    log_w = -(jnp.maximum(pre, 0.0) + jnp.log(1.0 + jnp.exp(-jnp.abs(pre)))) - 0.5
    ld_s[...] = -jnp.exp(log_w)
    eta = jax.nn.sigmoid(a0_ref[...] + _dot(x_wa, a2_ref[...]))
    gate = _dot(jax.nn.sigmoid(x_g), g2_ref[...])
    kk = x_k * kk_ref[...]
    kk_norm = jnp.sqrt(_head_reduce(kk * kk, ones_bd, two_pass=True))
    kf = x_k * (1.0 + (eta - 1.0) * ka_ref[...])
    bonus = _head_reduce(x_r * kf * rk_ref[...], ones_bd, two_pass=True) * x_v
    r_s[...] = x_r
    kf_s[...] = kf
    v_s[...] = x_v
    kn_s[...] = kk / jnp.maximum(kk_norm, KK_NORM_FLOOR)
    eta_s[...] = eta

    lane = lax.broadcasted_iota(jnp.int32, (RW_CHUNK, LANES), 1)
    first_head = lane < HEAD_DIM
    ri = lax.broadcasted_iota(jnp.int32, (LANES, LANES), 0)
    ci = lax.broadcasted_iota(jnp.int32, (LANES, LANES), 1)
    same_head = (ri < HEAD_DIM) == (ci < HEAD_DIM)
    strict_lower = same_head & (ci < ri)
    lower = same_head & (ci <= ri)
    eye = (ri == ci).astype(F32)
    tri = tri_ref[...]

    def stack_heads(t):
        return jnp.concatenate([jnp.where(first_head, t, 0.0), jnp.where(first_head, 0.0, t)],
                               axis=0)

    n_pairs = RW_WIDTH // LANES

    def chunk_operands(rows):
        ld = ld_s[rows, :]
        cum = _prefix_sum(tri, ld)
        total = cum[RW_CHUNK - 1:RW_CHUNK, :]
        e_incl = jnp.exp(cum)
        e_inv = jnp.exp(-cum)
        e_tail = jnp.exp(total - cum)
        w_total = jnp.exp(total)
        kn = kn_s[rows, :]
        kf_c = kf_s[rows, :]
        b = kn * eta_s[rows, :]
        a_dec = -kn * jnp.exp(cum - ld)
        r_dec = r_s[rows, :] * e_incl
        b_inv = b * e_inv
        k_inv = kf_c * e_inv
        b_tail = b * e_tail
        k_tail = kf_c * e_tail
        v_c = v_s[rows, :]
        units = []
        for p in range(n_pairs):
            cols = slice(p * LANES, (p + 1) * LANES)
            units.append(dict(
                a_sm=stack_heads(a_dec[:, cols]), r_sm=stack_heads(r_dec[:, cols]),
                v_sm=stack_heads(v_c[:, cols]), b_inv=b_inv[:, cols], k_inv=k_inv[:, cols],
                b_tail_t=stack_heads(b_tail[:, cols]).T, k_tail_t=stack_heads(k_tail[:, cols]).T,
                decay_diag=eye * w_total[:, cols]))
        return units

    def chunk_group(g, _):
        row_slices = [pl.ds(pl.multiple_of((g * RW_GROUP + c) * RW_CHUNK, RW_CHUNK), RW_CHUNK)
                      for c in range(RW_GROUP)]
        units = [u for rows in row_slices for u in chunk_operands(rows)]
        scores = [_dot_nt(jnp.concatenate([u["a_sm"], u["r_sm"]], axis=0),
                          jnp.concatenate([u["b_inv"], u["b_inv"], u["k_inv"], u["k_inv"]],
                                          axis=0)) for u in units]
        l_ab = [jnp.where(strict_lower, s[:LANES, :LANES], 0.0) for s in scores]
        l_ak = [jnp.where(strict_lower, s[:LANES, LANES:], 0.0) for s in scores]
        m_rb = [jnp.where(lower, s[LANES:, :LANES], 0.0) for s in scores]
        m_rk = [jnp.where(lower, s[LANES:, LANES:], 0.0) for s in scores]
        lakv = [_dot(l, u["v_sm"]) for l, u in zip(l_ak, units)]
        inv = [eye + l for l in l_ab]
        power = l_ab
        for _ in range(5):
            power = [_dot(pw, pw) for pw in power]
            inv = [iv + _dot(iv, pw) for iv, pw in zip(inv, power)]
        pu = [_dot(iv, jnp.concatenate([u["a_sm"], lv], axis=1))
              for iv, u, lv in zip(inv, units, lakv)]
        qy = [_dot(m, x) for m, x in zip(m_rb, pu)]
        y0 = [_dot(m, u["v_sm"]) for m, u in zip(m_rk, units)]
        gh = [_dot(u["b_tail_t"], x) for u, x in zip(units, pu)]
        kv = [_dot(u["k_tail_t"], u["v_sm"]) for u in units]
        states = [state_ref[p] for p in range(n_pairs)]
        for c, rows in enumerate(row_slices):
            new_states = []
            for p in range(n_pairs):
                i = c * n_pairs + p
                q_sm = units[i]["r_sm"] + qy[i][:, :LANES]
                y_sm = _dot(q_sm, states[p]) + (qy[i][:, LANES:] + y0[i])
                g_bd = gh[i][:, :LANES] + units[i]["decay_diag"]
                new_states.append(_dot(g_bd, states[p]) + (gh[i][:, LANES:] + kv[i]))
                y_s[rows, p * LANES:(p + 1) * LANES] = y_sm[:RW_CHUNK, :] + y_sm[RW_CHUNK:, :]
            states = new_states
        for p in range(n_pairs):
            state_ref[p] = states[p]
        return 0

    lax.fori_loop(0, n_rows // (RW_CHUNK * RW_GROUP), chunk_group, 0)

    y = y_s[...]
    centered = y - _head_reduce(y, ones_bd, two_pass=True) * (1.0 / HEAD_DIM)
    var = _head_reduce(centered * centered, ones_bd, two_pass=True) * (1.0 / HEAD_DIM)
    yn = centered * lax.rsqrt(var + LNX_EPS) * lnw_ref[...] + lnb_ref[...]
    o_ref[...] = ((yn + bonus) * gate).astype(o_ref.dtype)


def _rwkv_call(rw_in, mu, w0, w2_pad, a0, a2_pad, g2, k_k, k_a, r_k, ln_w, ln_b, ones_bd, tri):
    b, s, width = rw_in.shape
    row_tile = 256 if s % 256 == 0 else s
    vec = lambda n: _const_spec((1, n))
    scratch_rows = pltpu.VMEM((row_tile, RW_WIDTH), F32)
    return pl.pallas_call(
        _rwkv_kernel,
        out_shape=jax.ShapeDtypeStruct((b, s, RW_WIDTH), BF16),
        grid=(b, s // row_tile),
        in_specs=[pl.BlockSpec((None, row_tile, width), lambda bi, j: (bi, j, 0)),
                  vec(width), vec(RW_WIDTH), _const_spec(w2_pad.shape), vec(RW_WIDTH),
                  _const_spec(a2_pad.shape), _const_spec(g2.shape), vec(RW_WIDTH),
                  vec(RW_WIDTH), vec(RW_WIDTH), vec(RW_WIDTH), vec(RW_WIDTH),
                  _const_spec(ones_bd.shape), _const_spec(tri.shape)],
        out_specs=pl.BlockSpec((None, row_tile, RW_WIDTH), lambda bi, j: (bi, j, 0)),
        scratch_shapes=[pltpu.VMEM((1, width), F32),
                        pltpu.VMEM((RW_WIDTH // LANES, LANES, LANES), F32)]
                       + [scratch_rows] * 7,
        compiler_params=pltpu.CompilerParams(
            dimension_semantics=("parallel", "arbitrary"), vmem_limit_bytes=VMEM_LIMIT),
        name="rwkv7",
    )(rw_in, mu, w0, w2_pad, a0, a2_pad, g2, k_k, k_a, r_k, ln_w, ln_b, ones_bd, tri)


def _head_block_diag(value, dtype):
    idx = jnp.arange(MXU_DIM) // HEAD_DIM
    return jnp.where(idx[:, None] == idx[None, :], value, 0.0).astype(dtype)


def kernel(x, norm_ffn1, ffn1_gate, ffn1_up, ffn1_down, norm_mix, w_in, sb_q_norm, sb_k_norm,
           sb_out_norm, rw_mu, rw_w0, rw_w2, rw_a0, rw_a2, rw_g2, rw_k_k, rw_k_a, rw_r_k,
           rw_ln_w, rw_ln_b, w_out, norm_ffn2, ffn2_gate, ffn2_up, ffn2_down):
    b, s, d = x.shape
    n = b * s
    heads = SB_WIDTH // HEAD_DIM
    mean_bd = _head_block_diag(1.0 / HEAD_DIM, BF16)
    ones_bd = _head_block_diag(1.0, BF16)
    tri = jnp.tril(jnp.ones((RW_CHUNK, RW_CHUNK), BF16))
    row = lambda t: t.reshape(1, -1)

    h = x.reshape(n, d)
    for l in range(norm_ffn1.shape[0]):
        h = _ffn_call(h, row(norm_ffn1[l]), ffn1_gate[l].astype(BF16), ffn1_up[l].astype(BF16),
                      ffn1_down[l].astype(BF16))

        w_in_l = w_in[l].astype(BF16)
        q, k, v_t, rw_in = _mix_in_call(
            h, row(norm_mix[l]), w_in_l, w_in_l[:, 2 * SB_WIDTH:3 * SB_WIDTH].T,
            row(jnp.tile(sb_q_norm[l], heads)), row(jnp.tile(sb_k_norm[l], heads)), mean_bd)

        o_sb = _sb_attn_call(q.reshape(b, s, SB_WIDTH), k.reshape(b, s, SB_WIDTH), v_t,
                             row(sb_out_norm[l]))

        zeros = jnp.zeros((DECAY_LORA, RW_WIDTH), F32)
        w2_pad = jnp.concatenate([rw_w2[l], zeros], axis=0)
        a2_pad = jnp.concatenate([zeros, rw_a2[l]], axis=0).astype(BF16)
        o_rw = _rwkv_call(rw_in.reshape(b, s, RW_PROJ), row(rw_mu[l]), row(rw_w0[l]), w2_pad,
                          row(rw_a0[l]), a2_pad, rw_g2[l].astype(BF16), row(rw_k_k[l]),
                          row(rw_k_a[l]), row(rw_r_k[l]), row(rw_ln_w[l]), row(rw_ln_b[l]),
                          ones_bd, tri)

        h = _ffn_call(h, row(norm_ffn2[l]), ffn2_gate[l].astype(BF16), ffn2_up[l].astype(BF16),
                      ffn2_down[l].astype(BF16),
                      proj=(o_sb.reshape(n, SB_WIDTH), o_rw.reshape(n, RW_WIDTH),
                            w_out[l].astype(BF16)))
    return h.reshape(b, s, d)
```

```python
import functools

import jax
import jax.numpy as jnp
from jax import lax
from jax.experimental import pallas as pl
from jax.experimental.pallas import tpu as pltpu

F32 = jnp.float32
BF16 = jnp.bfloat16

HEAD_DIM = 64
SB_WIDTH = 512
RW_WIDTH = 512
DECAY_LORA = 64
AAA_LORA = 64
GATE_LORA = 128
RW_PROJ = 3 * RW_WIDTH + DECAY_LORA + AAA_LORA + GATE_LORA
RMS_EPS = 1e-6
LNX_EPS = 64e-5
KK_NORM_FLOOR = 1e-12

LANES = 128
MXU_DIM = 256
RW_CHUNK = 64
RW_GROUP = 2
SB_TILE = 256
SB_QTILE = 2 * SB_TILE
MASKED_LOG_BETA = -1e30
VMEM_LIMIT = 56 * 1024 * 1024


def _const_spec(shape):
    zeros = (0,) * len(shape)
    return pl.BlockSpec(shape, lambda *_: zeros, pipeline_mode=pl.Buffered(1))


def _dot(a, b):
    return jnp.dot(a.astype(BF16), b.astype(BF16), preferred_element_type=F32)


def _dot_nt(a, b):
    return lax.dot_general(a.astype(BF16), b.astype(BF16), (((1,), (1,)), ((), ())),
                           preferred_element_type=F32)


def _split_bf16(x):
    hi = x.astype(BF16)
    return hi, (x - hi.astype(F32)).astype(BF16)


def _dot_split(a, b_hi, b_lo):
    a_hi, a_lo = _split_bf16(a)
    return (jnp.dot(a_hi, b_hi, preferred_element_type=F32)
            + jnp.dot(a_hi, b_lo, preferred_element_type=F32)
            + jnp.dot(a_lo, b_hi, preferred_element_type=F32))


def _prefix_sum(tri, x):
    hi = x.astype(BF16)
    rest = x - hi.astype(F32)
    mid = rest.astype(BF16)
    lo = (rest - mid.astype(F32)).astype(BF16)
    return (jnp.dot(tri, hi, preferred_element_type=F32)
            + jnp.dot(tri, mid, preferred_element_type=F32)
            + jnp.dot(tri, lo, preferred_element_type=F32))


def _rms_norm_rows(x, gain):
    ms = jnp.mean(x * x, axis=-1, keepdims=True)
    return x * lax.rsqrt(ms + RMS_EPS) * gain


def _head_reduce(t, bd, two_pass):
    outs = []
    for half in range(t.shape[1] // MXU_DIM):
        th = t[:, half * MXU_DIM:(half + 1) * MXU_DIM]
        hi = th.astype(BF16)
        red = jnp.dot(hi, bd, preferred_element_type=F32)
        if two_pass:
            lo = (th - hi.astype(F32)).astype(BF16)
            red = red + jnp.dot(lo, bd, preferred_element_type=F32)
        outs.append(red)
    return jnp.concatenate(outs, axis=1)


def _swiglu_residual(x, gain_ref, wg_ref, wu_ref, wd_ref, ff_tile):
    h = _rms_norm_rows(x, gain_ref[...]).astype(BF16)
    acc = x
    for start in range(0, wg_ref.shape[1], ff_tile):
        cols = slice(start, min(start + ff_tile, wg_ref.shape[1]))
        gate = jnp.dot(h, wg_ref[:, cols], preferred_element_type=F32)
        up = jnp.dot(h, wu_ref[:, cols], preferred_element_type=F32)
        act = (gate * jax.nn.sigmoid(gate) * up * 0.5).astype(BF16)
        acc = acc + jnp.dot(act, wd_ref[cols, :], preferred_element_type=F32)
    return acc


def _ffn_kernel(x_ref, gain_ref, wg_ref, wu_ref, wd_ref, o_ref, *, ff_tile):
    o_ref[...] = _swiglu_residual(x_ref[...], gain_ref, wg_ref, wu_ref, wd_ref, ff_tile)


def _proj_ffn_kernel(x_ref, osb_ref, orw_ref, wo_ref, gain_ref, wg_ref, wu_ref, wd_ref,
                     o_ref, *, ff_tile):
    mixed = (jnp.dot(osb_ref[...], wo_ref[:SB_WIDTH, :], preferred_element_type=F32)
             + jnp.dot(orw_ref[...], wo_ref[SB_WIDTH:, :], preferred_element_type=F32))
    x = mixed + x_ref[...]
    o_ref[...] = _swiglu_residual(x, gain_ref, wg_ref, wu_ref, wd_ref, ff_tile)


def _ffn_tiles(n_tokens, d_ff):
    row_tile = 512 if n_tokens % 512 == 0 else n_tokens
    ff_tile = -(-d_ff // (2 * MXU_DIM)) * MXU_DIM if d_ff % MXU_DIM == 0 else d_ff
    return row_tile, ff_tile


def _ffn_call(x, gain, wg, wu, wd, proj=None):
    n, d = x.shape
    d_ff = wg.shape[1]
    row_tile, ff_tile = _ffn_tiles(n, d_ff)
    row_spec = pl.BlockSpec((row_tile, d), lambda i: (i, 0))
    weight_specs = [_const_spec((1, d)), _const_spec((d, d_ff)), _const_spec((d, d_ff)),
                    _const_spec((d_ff, d))]
    if proj is None:
        body = functools.partial(_ffn_kernel, ff_tile=ff_tile)
        in_specs = [row_spec] + weight_specs
        args = (x, gain, wg, wu, wd)
        name = "ffn1"
    else:
        o_sb, o_rw, w_out = proj
        body = functools.partial(_proj_ffn_kernel, ff_tile=ff_tile)
        in_specs = [row_spec,
                    pl.BlockSpec((row_tile, SB_WIDTH), lambda i: (i, 0)),
                    pl.BlockSpec((row_tile, RW_WIDTH), lambda i: (i, 0)),
                    _const_spec(w_out.shape)] + weight_specs
        args = (x, o_sb, o_rw, w_out, gain, wg, wu, wd)
        name = "ffn2"
    return pl.pallas_call(
        body,
        out_shape=jax.ShapeDtypeStruct((n, d), F32),
        grid=(n // row_tile,),
        in_specs=in_specs,
        out_specs=row_spec,
        compiler_params=pltpu.CompilerParams(
            dimension_semantics=("parallel",), vmem_limit_bytes=VMEM_LIMIT),
        name=name,
    )(*args)


def _mix_in_kernel(x_ref, gain_ref, w_ref, wvt_ref, qg_ref, kg_ref, mean_bd_ref,
                   q_ref, k_ref, vt_ref, rw_ref):
    h = _rms_norm_rows(x_ref[...], gain_ref[...]).astype(BF16)
    mean_bd = mean_bd_ref[...]

    def head_norm(t, gain):
        ms = _head_reduce(t * t, mean_bd, two_pass=False)
        return t * lax.rsqrt(ms + RMS_EPS) * gain

    q = jnp.dot(h, w_ref[:, 0:SB_WIDTH], preferred_element_type=F32)
    q_ref[...] = (head_norm(q, qg_ref[...]) * (HEAD_DIM ** -0.5)).astype(BF16)
    k = jnp.dot(h, w_ref[:, SB_WIDTH:2 * SB_WIDTH], preferred_element_type=F32)
    k_ref[...] = head_norm(k, kg_ref[...]).astype(BF16)
    vt = _dot_nt(wvt_ref[...], h).astype(BF16)
    for t in range(vt_ref.shape[0]):
        vt_ref[t] = vt[:, t * SB_TILE:(t + 1) * SB_TILE]
    rw_ref[...] = jnp.dot(h, w_ref[:, 3 * SB_WIDTH:], preferred_element_type=F32)


def _mix_in_call(x, gain, w_in, wv_t, q_gain, k_gain, mean_bd):
    n, d = x.shape
    row_tile = 2 * SB_TILE
    assert n % row_tile == 0
    row = lambda width: pl.BlockSpec((row_tile, width), lambda i: (i, 0))
    return pl.pallas_call(
        _mix_in_kernel,
        out_shape=(jax.ShapeDtypeStruct((n, SB_WIDTH), BF16),
                   jax.ShapeDtypeStruct((n, SB_WIDTH), BF16),
                   jax.ShapeDtypeStruct((n // SB_TILE, SB_WIDTH, SB_TILE), BF16),
                   jax.ShapeDtypeStruct((n, RW_PROJ), F32)),
        grid=(n // row_tile,),
        in_specs=[row(d), _const_spec((1, d)), _const_spec(w_in.shape),
                  _const_spec(wv_t.shape), _const_spec((1, SB_WIDTH)),
                  _const_spec((1, SB_WIDTH)), _const_spec(mean_bd.shape)],
        out_specs=(row(SB_WIDTH), row(SB_WIDTH),
                   pl.BlockSpec((row_tile // SB_TILE, SB_WIDTH, SB_TILE), lambda i: (i, 0, 0)),
                   row(RW_PROJ)),
        compiler_params=pltpu.CompilerParams(
            dimension_semantics=("parallel",), vmem_limit_bytes=VMEM_LIMIT),
        name="mix_in",
    )(x, gain, w_in, wv_t, q_gain, k_gain, mean_bd)


def _sb_attn_kernel(q_ref, k_ref, vt_ref, gain_ref, o_ref,
                    later_ref, acc_ref, z_ref, lb_ref, l1m_ref, row0_ref):
    tile = pl.program_id(2)
    n_tiles = 2 * tile + 2
    q_t = q_ref[...].astype(F32).T
    first_head = lax.broadcasted_iota(jnp.int32, q_t.shape, 0) < HEAD_DIM
    q_heads = jnp.concatenate([jnp.where(first_head, q_t, 0.0), jnp.where(first_head, 0.0, q_t)],
                              axis=1).astype(BF16)
    kk = lax.broadcasted_iota(jnp.int32, (SB_TILE, SB_TILE), 0)
    jj = lax.broadcasted_iota(jnp.int32, (SB_TILE, SB_TILE), 1)
    later_keys = jnp.where(jj > kk, 1.0, 0.0).astype(BF16)

    def causal_mask(key_offset):
        key_pos = lax.broadcasted_iota(jnp.int32, (SB_TILE, 2 * SB_QTILE), 0) + key_offset
        col = lax.broadcasted_iota(jnp.int32, (SB_TILE, 2 * SB_QTILE), 1)
        return key_pos < jnp.where(col >= SB_QTILE, col - SB_QTILE, col)

    def key_tile(t):
        return jnp.maximum(n_tiles - 1 - t, 0)

    def scores(t):
        start = pl.multiple_of(key_tile(t) * SB_TILE, SB_TILE)
        return jnp.dot(k_ref[pl.ds(start, SB_TILE), :], q_heads, preferred_element_type=F32)

    def stage1(z, mask=None):
        neg_part = jnp.minimum(z, 0.0)
        neg_relu = neg_part - z
        soft = jnp.log(1.0 + jnp.exp(neg_part + neg_relu))
        log_beta = neg_part - soft
        log_1m_beta = neg_relu - soft
        if mask is not None:
            log_beta = jnp.where(mask, log_beta, MASKED_LOG_BETA)
            log_1m_beta = jnp.where(mask, log_1m_beta, 0.0)
        return log_beta, log_1m_beta.astype(BF16), log_1m_beta[0:1, :]

    def stage2(t, slot, sums):
        v_t = vt_ref[key_tile(t)]
        later = later_ref[...]
        att = jnp.exp(lb_ref[slot] + sums + later).astype(BF16)
        for h in range(2):
            rows = slice(h * HEAD_DIM, (h + 1) * HEAD_DIM)
            acc_ref[rows, :] += jnp.dot(v_t[rows, :], att[:, h * SB_QTILE:(h + 1) * SB_QTILE],
                                        preferred_element_type=F32)
        later_ref[...] = later + sums[0:1, :] + row0_ref[slot]

    def trip(t, cur, mask=None):
        other = 1 - cur
        z_next = scores(t + 1)
        sums = jnp.dot(later_keys, l1m_ref[other], preferred_element_type=F32)
        log_beta, log_1m_beta, key0_row = stage1(z_ref[cur], mask)
        stage2(t - 1, other, sums)
        lb_ref[cur] = log_beta
        l1m_ref[cur] = log_1m_beta
        row0_ref[cur] = key0_row
        z_ref[other] = z_next

    later_ref[...] = jnp.zeros_like(later_ref)
    acc_ref[...] = jnp.zeros_like(acc_ref)
    lb_ref[0], l1m_ref[0], row0_ref[0] = stage1(scores(0), causal_mask(SB_TILE))
    z_ref[1] = scores(1)
    trip(1, 1, causal_mask(0))

    def trip_pair(i, _):
        trip(2 * i + 2, 0)
        trip(2 * i + 3, 1)
        return 0

    lax.fori_loop(0, tile, trip_pair, 0)
    stage2(n_tiles - 1, 1, jnp.dot(later_keys, l1m_ref[1], preferred_element_type=F32))

    def head_norm(acc):
        ms = jnp.sum(acc * acc, axis=0, keepdims=True) * (1.0 / HEAD_DIM)
        return acc * lax.rsqrt(ms + RMS_EPS)

    out_t = jnp.concatenate([head_norm(acc_ref[:HEAD_DIM, :]), head_norm(acc_ref[HEAD_DIM:, :])],
                            axis=0)
    o_ref[...] = (out_t.T * gain_ref[...]).astype(o_ref.dtype)


def _sb_attn_call(q, k, v_t, out_gain):
    b, s, width = q.shape
    pairs = width // LANES
    assert s % SB_QTILE == 0
    blk_spec = pl.BlockSpec((None, SB_QTILE, LANES), lambda bi, p, i: (bi, i, p))
    both_heads = 2 * SB_QTILE
    return pl.pallas_call(
        _sb_attn_kernel,
        out_shape=jax.ShapeDtypeStruct((b, s, width), BF16),
        grid=(b, pairs, s // SB_QTILE),
        in_specs=[blk_spec,
                  pl.BlockSpec((None, s, LANES), lambda bi, p, i: (bi, 0, p)),
                  pl.BlockSpec((s // SB_TILE, LANES, SB_TILE), lambda bi, p, i: (bi, p, 0)),
                  pl.BlockSpec((1, LANES), lambda bi, p, i: (0, p))],
        out_specs=blk_spec,
        scratch_shapes=[pltpu.VMEM((1, both_heads), F32),
                        pltpu.VMEM((LANES, SB_QTILE), F32),
                        pltpu.VMEM((2, SB_TILE, both_heads), F32),
                        pltpu.VMEM((2, SB_TILE, both_heads), F32),
                        pltpu.VMEM((2, SB_TILE, both_heads), BF16),
                        pltpu.VMEM((2, 1, both_heads), F32)],
        compiler_params=pltpu.CompilerParams(
            dimension_semantics=("parallel", "parallel", "arbitrary"),
            vmem_limit_bytes=VMEM_LIMIT),
        name="sb_attn",
    )(q, k, v_t, out_gain)


def _rwkv_kernel(u_ref, mu_ref, w0_ref, w2_ref, a0_ref, a2_ref, g2_ref, kk_ref, ka_ref,
                 rk_ref, lnw_ref, lnb_ref, ones_bd_ref, tri_ref, o_ref,
                 prev_ref, state_ref, r_s, kf_s, v_s, kn_s, eta_s, ld_s, y_s):
    n_rows = u_ref.shape[0]

    @pl.when(pl.program_id(1) == 0)
    def _():
        prev_ref[...] = jnp.zeros_like(prev_ref)
        state_ref[...] = jnp.zeros_like(state_ref)

    u = u_ref[...]
    row_id = lax.broadcasted_iota(jnp.int32, u.shape, 0)
    shifted = jnp.where(row_id == 0, prev_ref[...], pltpu.roll(u, 1, axis=0))
    prev_ref[...] = u[n_rows - 1:n_rows, :]
    u = u + (shifted - u) * mu_ref[...]

    x_r = u[:, 0:RW_WIDTH]
    x_k = u[:, RW_WIDTH:2 * RW_WIDTH]
    x_v = u[:, 2 * RW_WIDTH:3 * RW_WIDTH]
    x_wa = u[:, 3 * RW_WIDTH:3 * RW_WIDTH + DECAY_LORA + AAA_LORA]
    x_g = u[:, 3 * RW_WIDTH + DECAY_LORA + AAA_LORA:]
    ones_bd = ones_bd_ref[...]

    pre = -(w0_ref[...] + _dot_split(jnp.tanh(x_wa), w2_ref[0], w2_ref[1]))
    log_w = -(jnp.maximum(pre, 0.0) + jnp.log(1.0 + jnp.exp(-jnp.abs(pre)))) - 0.5
    ld_s[...] = -jnp.exp(log_w)
    eta = jax.nn.sigmoid(a0_ref[...] + _dot(x_wa, a2_ref[...]))
    gate = _dot(jax.nn.sigmoid(x_g), g2_ref[...])
    kk = x_k * kk_ref[...]
    kk_scale = jnp.minimum(lax.rsqrt(_head_reduce(kk * kk, ones_bd, two_pass=True)),
                           1.0 / KK_NORM_FLOOR)
    kf = x_k * (1.0 + (eta - 1.0) * ka_ref[...])
    bonus = _head_reduce(x_r * kf * rk_ref[...], ones_bd, two_pass=False) * x_v
    r_s[...] = x_r
    kf_s[...] = kf
    v_s[...] = x_v
    kn_s[...] = kk * kk_scale
    eta_s[...] = eta

    lane = lax.broadcasted_iota(jnp.int32, (RW_CHUNK, LANES), 1)
    first_head = lane < HEAD_DIM
    ri = lax.broadcasted_iota(jnp.int32, (LANES, LANES), 0)
    ci = lax.broadcasted_iota(jnp.int32, (LANES, LANES), 1)
    same_head = (ri < HEAD_DIM) == (ci < HEAD_DIM)
    strict_lower = same_head & (ci < ri)
    lower = same_head & (ci <= ri)
    eye = (ri == ci).astype(F32)
    tri = tri_ref[...]

    def stack_heads(t):
        return jnp.concatenate([jnp.where(first_head, t, 0.0), jnp.where(first_head, 0.0, t)],
                               axis=0)

    n_pairs = RW_WIDTH // LANES

    def chunk_operands(rows):
        ld = ld_s[rows, :]
        cum = _prefix_sum(tri, ld)
        total = cum[RW_CHUNK - 1:RW_CHUNK, :]
        e_incl = jnp.exp(cum)
        e_inv = jnp.exp(-cum)
        e_tail = jnp.exp(total - cum)
        w_total = jnp.exp(total)
        kn = kn_s[rows, :]
        kf_c = kf_s[rows, :]
        b = kn * eta_s[rows, :]
        a_dec = -kn * jnp.exp(cum - ld)
        r_dec = r_s[rows, :] * e_incl
        b_inv = b * e_inv
        k_inv = kf_c * e_inv
        b_tail = b * e_tail
        k_tail = kf_c * e_tail
        v_c = v_s[rows, :]
        units = []
        for p in range(n_pairs):
            cols = slice(p * LANES, (p + 1) * LANES)
            units.append(dict(
                a_sm=stack_heads(a_dec[:, cols]), r_sm=stack_heads(r_dec[:, cols]),
                v_sm=stack_heads(v_c[:, cols]), b_inv=b_inv[:, cols], k_inv=k_inv[:, cols],
                b_tail_t=stack_heads(b_tail[:, cols]).T, k_tail_t=stack_heads(k_tail[:, cols]).T,
                decay_diag=eye * w_total[:, cols]))
        return units

    def chunk_group(g, _):
        row_slices = [pl.ds(pl.multiple_of((g * RW_GROUP + c) * RW_CHUNK, RW_CHUNK), RW_CHUNK)
                      for c in range(RW_GROUP)]
        units = [u for rows in row_slices for u in chunk_operands(rows)]
        scores = [_dot_nt(jnp.concatenate([u["a_sm"], u["r_sm"]], axis=0),
                          jnp.concatenate([u["b_inv"], u["b_inv"], u["k_inv"], u["k_inv"]],
                                          axis=0)) for u in units]
        l_ab = [jnp.where(strict_lower, s[:LANES, :LANES], 0.0) for s in scores]
        l_ak = [jnp.where(strict_lower, s[:LANES, LANES:], 0.0) for s in scores]
        m_rb = [jnp.where(lower, s[LANES:, :LANES], 0.0) for s in scores]
        m_rk = [jnp.where(lower, s[LANES:, LANES:], 0.0) for s in scores]
        lakv = [_dot(l, u["v_sm"]) for l, u in zip(l_ak, units)]
        inv = [eye + l for l in l_ab]
        power = [_dot(l, l) for l in l_ab]
        for _ in range(4):
            both = [_dot(pw, jnp.concatenate([pw, iv], axis=1)) for pw, iv in zip(power, inv)]
            inv = [iv + x[:, LANES:] for iv, x in zip(inv, both)]
            power = [x[:, :LANES] for x in both]
        inv = [iv + _dot(pw, iv) for iv, pw in zip(inv, power)]
        pu = [_dot(iv, jnp.concatenate([u["a_sm"], lv], axis=1))
              for iv, u, lv in zip(inv, units, lakv)]
        zeros = jnp.zeros((LANES, LANES), F32)
        tail = [_dot(jnp.concatenate([jnp.concatenate([mb, mk], axis=1),
                                      jnp.concatenate([u["b_tail_t"], u["k_tail_t"]], axis=1)],
                                     axis=0),
                     jnp.concatenate([x, jnp.concatenate([zeros, u["v_sm"]], axis=1)], axis=0))
                for mb, mk, u, x in zip(m_rb, m_rk, units, pu)]
        states = [state_ref[p] for p in range(n_pairs)]
        for c, rows in enumerate(row_slices):
            new_states = []
            for p in range(n_pairs):
                i = c * n_pairs + p
                q_sm = units[i]["r_sm"] + tail[i][:LANES, :LANES]
                y_sm = _dot(q_sm, states[p]) + tail[i][:LANES, LANES:]
                g_bd = tail[i][LANES:, :LANES] + units[i]["decay_diag"]
                new_states.append(_dot(g_bd, states[p]) + tail[i][LANES:, LANES:])
                y_s[rows, p * LANES:(p + 1) * LANES] = y_sm[:RW_CHUNK, :] + y_sm[RW_CHUNK:, :]
            states = new_states
        for p in range(n_pairs):
            state_ref[p] = states[p]
        return 0

    lax.fori_loop(0, n_rows // (RW_CHUNK * RW_GROUP), chunk_group, 0)

    y = y_s[...]
    centered = y - _head_reduce(y, ones_bd, two_pass=False) * (1.0 / HEAD_DIM)
    var = _head_reduce(centered * centered, ones_bd, two_pass=False) * (1.0 / HEAD_DIM)
    yn = centered * lax.rsqrt(var + LNX_EPS) * lnw_ref[...] + lnb_ref[...]
    o_ref[...] = ((yn + bonus) * gate).astype(o_ref.dtype)


def _rwkv_call(rw_in, mu, w0, w2_pad, a0, a2_pad, g2, k_k, k_a, r_k, ln_w, ln_b, ones_bd, tri):
    b, s, width = rw_in.shape
    row_tile = 256 if s % 256 == 0 else s
    vec = lambda n: _const_spec((1, n))
    scratch_rows = pltpu.VMEM((row_tile, RW_WIDTH), F32)
    return pl.pallas_call(
        _rwkv_kernel,
        out_shape=jax.ShapeDtypeStruct((b, s, RW_WIDTH), BF16),
        grid=(b, s // row_tile),
        in_specs=[pl.BlockSpec((None, row_tile, width), lambda bi, j: (bi, j, 0)),
                  vec(width), vec(RW_WIDTH), _const_spec(w2_pad.shape), vec(RW_WIDTH),
                  _const_spec(a2_pad.shape), _const_spec(g2.shape), vec(RW_WIDTH),
                  vec(RW_WIDTH), vec(RW_WIDTH), vec(RW_WIDTH), vec(RW_WIDTH),
                  _const_spec(ones_bd.shape), _const_spec(tri.shape)],
        out_specs=pl.BlockSpec((None, row_tile, RW_WIDTH), lambda bi, j: (bi, j, 0)),
        scratch_shapes=[pltpu.VMEM((1, width), F32),
                        pltpu.VMEM((RW_WIDTH // LANES, LANES, LANES), F32)]
                       + [scratch_rows] * 7,
        compiler_params=pltpu.CompilerParams(
            dimension_semantics=("parallel", "arbitrary"), vmem_limit_bytes=VMEM_LIMIT),
        name="rwkv7",
    )(rw_in, mu, w0, w2_pad, a0, a2_pad, g2, k_k, k_a, r_k, ln_w, ln_b, ones_bd, tri)


def _head_block_diag(value, dtype):
    idx = jnp.arange(MXU_DIM) // HEAD_DIM
    return jnp.where(idx[:, None] == idx[None, :], value, 0.0).astype(dtype)


def kernel(x, norm_ffn1, ffn1_gate, ffn1_up, ffn1_down, norm_mix, w_in, sb_q_norm, sb_k_norm,
           sb_out_norm, rw_mu, rw_w0, rw_w2, rw_a0, rw_a2, rw_g2, rw_k_k, rw_k_a, rw_r_k,
           rw_ln_w, rw_ln_b, w_out, norm_ffn2, ffn2_gate, ffn2_up, ffn2_down):
    b, s, d = x.shape
    n = b * s
    heads = SB_WIDTH // HEAD_DIM
    mean_bd = _head_block_diag(1.0 / HEAD_DIM, BF16)
    ones_bd = _head_block_diag(1.0, BF16)
    tri = jnp.tril(jnp.ones((RW_CHUNK, RW_CHUNK), BF16))
    row = lambda t: t.reshape(1, -1)

    h = x.reshape(n, d)
    for l in range(norm_ffn1.shape[0]):
        h = _ffn_call(h, row(norm_ffn1[l]), ffn1_gate[l].astype(BF16), ffn1_up[l].astype(BF16),
                      ffn1_down[l].astype(BF16))

        w_in_l = w_in[l].astype(BF16)
        q, k, v_t, rw_in = _mix_in_call(
            h, row(norm_mix[l]), w_in_l, w_in_l[:, 2 * SB_WIDTH:3 * SB_WIDTH].T,
            row(jnp.tile(sb_q_norm[l], heads)), row(jnp.tile(sb_k_norm[l], heads)), mean_bd)

        o_sb = _sb_attn_call(q.reshape(b, s, SB_WIDTH), k.reshape(b, s, SB_WIDTH), v_t,
                             row(sb_out_norm[l]))

        zeros = jnp.zeros((DECAY_LORA, RW_WIDTH), F32)
        w2_pad = jnp.stack(_split_bf16(jnp.concatenate([rw_w2[l], zeros], axis=0)))
        a2_pad = jnp.concatenate([zeros, rw_a2[l]], axis=0).astype(BF16)
        o_rw = _rwkv_call(rw_in.reshape(b, s, RW_PROJ), row(rw_mu[l]), row(rw_w0[l]), w2_pad,
                          row(rw_a0[l]), a2_pad, rw_g2[l].astype(BF16), row(rw_k_k[l]),
                          row(rw_k_a[l]), row(rw_r_k[l]), row(rw_ln_w[l]), row(rw_ln_b[l]),
                          ones_bd, tri)

        h = _ffn_call(h, row(norm_ffn2[l]), ffn2_gate[l].astype(BF16), ffn2_up[l].astype(BF16),
                      ffn2_down[l].astype(BF16),
                      proj=(o_sb.reshape(n, SB_WIDTH), o_rw.reshape(n, RW_WIDTH),
                            w_out[l].astype(BF16)))
    return h.reshape(b, s, d)
```

```python
import functools

import jax
import jax.numpy as jnp
from jax import lax
from jax.experimental import pallas as pl
from jax.experimental.pallas import tpu as pltpu

F32 = jnp.float32
BF16 = jnp.bfloat16

HEAD_DIM = 64
SB_WIDTH = 512
RW_WIDTH = 512
DECAY_LORA = 64
AAA_LORA = 64
GATE_LORA = 128
RW_PROJ = 3 * RW_WIDTH + DECAY_LORA + AAA_LORA + GATE_LORA
RMS_EPS = 1e-6
LNX_EPS = 64e-5
KK_NORM_FLOOR = 1e-12

LANES = 128
MXU_DIM = 256
RW_CHUNK = 64
RW_GROUP = 2
SB_TILE = 256
SB_QTILE = 2 * SB_TILE
MASKED_LOG_BETA = -1e30
VMEM_LIMIT = 56 * 1024 * 1024


def _const_spec(shape):
    zeros = (0,) * len(shape)
    return pl.BlockSpec(shape, lambda *_: zeros, pipeline_mode=pl.Buffered(1))


def _dot(a, b):
    return jnp.dot(a.astype(BF16), b.astype(BF16), preferred_element_type=F32)


def _dot_nt(a, b):
    return lax.dot_general(a.astype(BF16), b.astype(BF16), (((1,), (1,)), ((), ())),
                           preferred_element_type=F32)


def _split_bf16(x):
    hi = x.astype(BF16)
    return hi, (x - hi.astype(F32)).astype(BF16)


def _dot_split(a, b_hi, b_lo):
    a_hi, a_lo = _split_bf16(a)
    return (jnp.dot(a_hi, b_hi, preferred_element_type=F32)
            + jnp.dot(a_hi, b_lo, preferred_element_type=F32)
            + jnp.dot(a_lo, b_hi, preferred_element_type=F32))


def _prefix_sum(tri, x):
    hi = x.astype(BF16)
    rest = x - hi.astype(F32)
    mid = rest.astype(BF16)
    lo = (rest - mid.astype(F32)).astype(BF16)
    return (jnp.dot(tri, hi, preferred_element_type=F32)
            + jnp.dot(tri, mid, preferred_element_type=F32)
            + jnp.dot(tri, lo, preferred_element_type=F32))


def _rms_norm_rows(x, gain):
    ms = jnp.mean(x * x, axis=-1, keepdims=True)
    return x * lax.rsqrt(ms + RMS_EPS) * gain


def _head_reduce(t, bd, two_pass):
    outs = []
    for half in range(t.shape[1] // MXU_DIM):
        th = t[:, half * MXU_DIM:(half + 1) * MXU_DIM]
        hi = th.astype(BF16)
        red = jnp.dot(hi, bd, preferred_element_type=F32)
        if two_pass:
            lo = (th - hi.astype(F32)).astype(BF16)
            red = red + jnp.dot(lo, bd, preferred_element_type=F32)
        outs.append(red)
    return jnp.concatenate(outs, axis=1)


def _swiglu_residual(x, gain_ref, wg_ref, wu_ref, wd_ref, ff_tile):
    h = _rms_norm_rows(x, gain_ref[...]).astype(BF16)
    acc = x
    for start in range(0, wg_ref.shape[1], ff_tile):
        cols = slice(start, min(start + ff_tile, wg_ref.shape[1]))
        gate = jnp.dot(h, wg_ref[:, cols], preferred_element_type=F32)
        up = jnp.dot(h, wu_ref[:, cols], preferred_element_type=F32)
        act = (gate * jax.nn.sigmoid(gate) * up * 0.5).astype(BF16)
        acc = acc + jnp.dot(act, wd_ref[cols, :], preferred_element_type=F32)
    return acc


def _ffn_kernel(x_ref, gain_ref, wg_ref, wu_ref, wd_ref, o_ref, *, ff_tile):
    o_ref[...] = _swiglu_residual(x_ref[...], gain_ref, wg_ref, wu_ref, wd_ref, ff_tile)


def _proj_ffn_kernel(x_ref, osb_ref, orw_ref, wo_ref, gain_ref, wg_ref, wu_ref, wd_ref,
                     o_ref, *, ff_tile):
    mixed = (jnp.dot(osb_ref[...], wo_ref[:SB_WIDTH, :], preferred_element_type=F32)
             + jnp.dot(orw_ref[...], wo_ref[SB_WIDTH:, :], preferred_element_type=F32))
    x = mixed + x_ref[...]
    o_ref[...] = _swiglu_residual(x, gain_ref, wg_ref, wu_ref, wd_ref, ff_tile)


def _ffn_tiles(n_tokens, d_ff):
    row_tile = 512 if n_tokens % 512 == 0 else n_tokens
    ff_tile = -(-d_ff // (2 * MXU_DIM)) * MXU_DIM if d_ff % MXU_DIM == 0 else d_ff
    return row_tile, ff_tile


def _ffn_call(x, gain, wg, wu, wd, proj=None):
    n, d = x.shape
    d_ff = wg.shape[1]
    row_tile, ff_tile = _ffn_tiles(n, d_ff)
    row_spec = pl.BlockSpec((row_tile, d), lambda i: (i, 0))
    weight_specs = [_const_spec((1, d)), _const_spec((d, d_ff)), _const_spec((d, d_ff)),
                    _const_spec((d_ff, d))]
    if proj is None:
        body = functools.partial(_ffn_kernel, ff_tile=ff_tile)
        in_specs = [row_spec] + weight_specs
        args = (x, gain, wg, wu, wd)
        name = "ffn1"
    else:
        o_sb, o_rw, w_out = proj
        body = functools.partial(_proj_ffn_kernel, ff_tile=ff_tile)
        in_specs = [row_spec,
                    pl.BlockSpec((row_tile, SB_WIDTH), lambda i: (i, 0)),
                    pl.BlockSpec((row_tile, RW_WIDTH), lambda i: (i, 0)),
                    _const_spec(w_out.shape)] + weight_specs
        args = (x, o_sb, o_rw, w_out, gain, wg, wu, wd)
        name = "ffn2"
    return pl.pallas_call(
        body,
        out_shape=jax.ShapeDtypeStruct((n, d), F32),
        grid=(n // row_tile,),
        in_specs=in_specs,
        out_specs=row_spec,
        compiler_params=pltpu.CompilerParams(
            dimension_semantics=("parallel",), vmem_limit_bytes=VMEM_LIMIT),
        name=name,
    )(*args)


def _mix_in_kernel(x_ref, gain_ref, w_ref, wvt_ref, qg_ref, kg_ref, mean_bd_ref,
                   q_ref, k_ref, vt_ref, rw_ref):
    h = _rms_norm_rows(x_ref[...], gain_ref[...]).astype(BF16)
    mean_bd = mean_bd_ref[...]

    def head_norm(t, gain):
        ms = _head_reduce(t * t, mean_bd, two_pass=False)
        return t * lax.rsqrt(ms + RMS_EPS) * gain

    q = jnp.dot(h, w_ref[:, 0:SB_WIDTH], preferred_element_type=F32)
    q_ref[...] = (head_norm(q, qg_ref[...]) * (HEAD_DIM ** -0.5)).astype(BF16)
    k = jnp.dot(h, w_ref[:, SB_WIDTH:2 * SB_WIDTH], preferred_element_type=F32)
    k_ref[...] = head_norm(k, kg_ref[...]).astype(BF16)
    vt = _dot_nt(wvt_ref[...], h).astype(BF16)
    for t in range(vt_ref.shape[0]):
        vt_ref[t] = vt[:, t * SB_TILE:(t + 1) * SB_TILE]
    rw_ref[...] = jnp.dot(h, w_ref[:, 3 * SB_WIDTH:], preferred_element_type=F32)


def _mix_in_call(x, gain, w_in, wv_t, q_gain, k_gain, mean_bd):
    n, d = x.shape
    row_tile = 2 * SB_TILE
    assert n % row_tile == 0
    row = lambda width: pl.BlockSpec((row_tile, width), lambda i: (i, 0))
    return pl.pallas_call(
        _mix_in_kernel,
        out_shape=(jax.ShapeDtypeStruct((n, SB_WIDTH), BF16),
                   jax.ShapeDtypeStruct((n, SB_WIDTH), BF16),
                   jax.ShapeDtypeStruct((n // SB_TILE, SB_WIDTH, SB_TILE), BF16),
                   jax.ShapeDtypeStruct((n, RW_PROJ), F32)),
        grid=(n // row_tile,),
        in_specs=[row(d), _const_spec((1, d)), _const_spec(w_in.shape),
                  _const_spec(wv_t.shape), _const_spec((1, SB_WIDTH)),
                  _const_spec((1, SB_WIDTH)), _const_spec(mean_bd.shape)],
        out_specs=(row(SB_WIDTH), row(SB_WIDTH),
                   pl.BlockSpec((row_tile // SB_TILE, SB_WIDTH, SB_TILE), lambda i: (i, 0, 0)),
                   row(RW_PROJ)),
        compiler_params=pltpu.CompilerParams(
            dimension_semantics=("parallel",), vmem_limit_bytes=VMEM_LIMIT),
        name="mix_in",
    )(x, gain, w_in, wv_t, q_gain, k_gain, mean_bd)


def _sb_attn_kernel(q_ref, k_ref, vt_ref, gain_ref, o_ref, *scratch):
    def query_tile(tile, _):
        rows = pl.ds(pl.multiple_of(tile * SB_QTILE, SB_QTILE), SB_QTILE)
        _sb_attn_tile(tile, q_ref.at[rows, :], k_ref, vt_ref, gain_ref, o_ref.at[rows, :],
                      *scratch)
        return 0

    lax.fori_loop(0, q_ref.shape[0] // SB_QTILE, query_tile, 0)


def _sb_attn_tile(tile, q_ref, k_ref, vt_ref, gain_ref, o_ref,
                  later_ref, acc_ref, z_ref, lb_ref, l1m_ref, row0_ref):
    n_tiles = 2 * tile + 2
    q_t = q_ref[...].astype(F32).T
    first_head = lax.broadcasted_iota(jnp.int32, q_t.shape, 0) < HEAD_DIM
    q_heads = jnp.concatenate([jnp.where(first_head, q_t, 0.0), jnp.where(first_head, 0.0, q_t)],
                              axis=1).astype(BF16)
    kk = lax.broadcasted_iota(jnp.int32, (SB_TILE, SB_TILE), 0)
    jj = lax.broadcasted_iota(jnp.int32, (SB_TILE, SB_TILE), 1)
    later_keys = jnp.where(jj > kk, 1.0, 0.0).astype(BF16)

    def causal_mask(key_offset, n_queries):
        key_pos = lax.broadcasted_iota(jnp.int32, (SB_TILE, 2 * n_queries), 0) + key_offset
        col = lax.broadcasted_iota(jnp.int32, (SB_TILE, 2 * n_queries), 1)
        return key_pos < jnp.where(col >= n_queries, col - n_queries, col)

    def key_tile(t):
        return jnp.maximum(n_tiles - 1 - t, 0)

    def scores(t):
        start = pl.multiple_of(key_tile(t) * SB_TILE, SB_TILE)
        return jnp.dot(k_ref[pl.ds(start, SB_TILE), :], q_heads, preferred_element_type=F32)

    def stage1(z, mask=None):
        neg_part = jnp.minimum(z, 0.0)
        neg_relu = neg_part - z
        soft = jnp.log(1.0 + jnp.exp(neg_part + neg_relu))
        log_beta = neg_part - soft
        log_1m_beta = neg_relu - soft
        if mask is not None:
            log_beta = jnp.where(mask, log_beta, MASKED_LOG_BETA)
            log_1m_beta = jnp.where(mask, log_1m_beta, 0.0)
        return log_beta, log_1m_beta.astype(BF16), log_1m_beta[0:1, :]

    def stage2(t, slot, sums):
        v_t = vt_ref[key_tile(t)]
        later = later_ref[...]
        att = jnp.exp(lb_ref[slot] + sums + later).astype(BF16)
        for h in range(2):
            rows = slice(h * HEAD_DIM, (h + 1) * HEAD_DIM)
            acc_ref[rows, :] += jnp.dot(v_t[rows, :], att[:, h * SB_QTILE:(h + 1) * SB_QTILE],
                                        preferred_element_type=F32)
        later_ref[...] = later + sums[0:1, :] + row0_ref[slot]

    def trip(t, cur):
        other = 1 - cur
        z_next = scores(t + 1)
        sums = jnp.dot(later_keys, l1m_ref[other], preferred_element_type=F32)
        log_beta, log_1m_beta, key0_row = stage1(z_ref[cur])
        stage2(t - 1, other, sums)
        lb_ref[cur] = log_beta
        l1m_ref[cur] = log_1m_beta
        row0_ref[cur] = key0_row
        z_ref[other] = z_next

    later_ref[...] = jnp.zeros_like(later_ref)
    acc_ref[...] = jnp.zeros_like(acc_ref)

    half = SB_QTILE // 2
    late_q = jnp.concatenate([q_heads[:, half:SB_QTILE], q_heads[:, SB_QTILE + half:]], axis=1)
    t0_rows = pl.ds(pl.multiple_of(key_tile(0) * SB_TILE, SB_TILE), SB_TILE)
    z_t0 = jnp.dot(k_ref[t0_rows, :], late_q, preferred_element_type=F32)
    z_t1 = scores(1)
    z_t2 = scores(2)
    lb_t0, l1m_t0, row0_t0 = stage1(z_t0, causal_mask(0, half))
    sums_t0 = jnp.dot(later_keys, l1m_t0, preferred_element_type=F32)
    lb_ref[1], l1m_ref[1], row0_ref[1] = stage1(z_t1, causal_mask(0, SB_QTILE))
    att_t0 = jnp.exp(lb_t0 + sums_t0).astype(BF16)
    v_t0 = vt_ref[key_tile(0)]
    carry_t0 = sums_t0[0:1, :] + row0_t0
    for h in range(2):
        rows = slice(h * HEAD_DIM, (h + 1) * HEAD_DIM)
        acc_ref[rows, half:] = jnp.dot(v_t0[rows, :], att_t0[:, h * half:(h + 1) * half],
                                       preferred_element_type=F32)
        later_ref[:, h * SB_QTILE + half:(h + 1) * SB_QTILE] = carry_t0[:, h * half:(h + 1) * half]
    z_ref[0] = z_t2

    def trip_pair(i, _):
        trip(2 * i + 2, 0)
        trip(2 * i + 3, 1)
        return 0

    lax.fori_loop(0, tile, trip_pair, 0)
    stage2(n_tiles - 1, 1, jnp.dot(later_keys, l1m_ref[1], preferred_element_type=F32))

    def head_norm(acc):
        ms = jnp.sum(acc * acc, axis=0, keepdims=True) * (1.0 / HEAD_DIM)
        return acc * lax.rsqrt(ms + RMS_EPS)

    out_t = jnp.concatenate([head_norm(acc_ref[:HEAD_DIM, :]), head_norm(acc_ref[HEAD_DIM:, :])],
                            axis=0)
    o_ref[...] = (out_t.T * gain_ref[...]).astype(o_ref.dtype)


def _sb_attn_call(q, k, v_t, out_gain):
    b, s, width = q.shape
    pairs = width // LANES
    assert s % SB_QTILE == 0
    seq_spec = pl.BlockSpec((None, s, LANES), lambda bi, p: (bi, 0, p))
    both_heads = 2 * SB_QTILE
    return pl.pallas_call(
        _sb_attn_kernel,
        out_shape=jax.ShapeDtypeStruct((b, s, width), BF16),
        grid=(b, pairs),
        in_specs=[seq_spec, seq_spec,
                  pl.BlockSpec((s // SB_TILE, LANES, SB_TILE), lambda bi, p: (bi, p, 0)),
                  pl.BlockSpec((1, LANES), lambda bi, p: (0, p))],
        out_specs=seq_spec,
        scratch_shapes=[pltpu.VMEM((1, both_heads), F32),
                        pltpu.VMEM((LANES, SB_QTILE), F32),
                        pltpu.VMEM((2, SB_TILE, both_heads), F32),
                        pltpu.VMEM((2, SB_TILE, both_heads), F32),
                        pltpu.VMEM((2, SB_TILE, both_heads), BF16),
                        pltpu.VMEM((2, 1, both_heads), F32)],
        compiler_params=pltpu.CompilerParams(
            dimension_semantics=("parallel", "parallel"), vmem_limit_bytes=VMEM_LIMIT),
        name="sb_attn",
    )(q, k, v_t, out_gain)


def _rwkv_kernel(u_ref, mu_ref, w0_ref, w2_ref, a0_ref, a2_ref, g2_ref, kk_ref, ka_ref,
                 rk_ref, lnw_ref, lnb_ref, ones_bd_ref, tri_ref, o_ref,
                 prev_ref, state_ref, r_s, kf_s, v_s, kn_s, eta_s, ld_s, y_s):
    n_rows = u_ref.shape[0]

    @pl.when(pl.program_id(1) == 0)
    def _():
        prev_ref[...] = jnp.zeros_like(prev_ref)
        state_ref[...] = jnp.zeros_like(state_ref)

    u = u_ref[...]
    row_id = lax.broadcasted_iota(jnp.int32, u.shape, 0)
    shifted = jnp.where(row_id == 0, prev_ref[...], pltpu.roll(u, 1, axis=0))
    prev_ref[...] = u[n_rows - 1:n_rows, :]
    u = u + (shifted - u) * mu_ref[...]

    x_r = u[:, 0:RW_WIDTH]
    x_k = u[:, RW_WIDTH:2 * RW_WIDTH]
    x_v = u[:, 2 * RW_WIDTH:3 * RW_WIDTH]
    x_wa = u[:, 3 * RW_WIDTH:3 * RW_WIDTH + DECAY_LORA + AAA_LORA]
    x_g = u[:, 3 * RW_WIDTH + DECAY_LORA + AAA_LORA:]
    ones_bd = ones_bd_ref[...]

    pre = -(w0_ref[...] + _dot_split(jnp.tanh(x_wa), w2_ref[0], w2_ref[1]))
    log_w = -(jnp.maximum(pre, 0.0) + jnp.log(1.0 + jnp.exp(-jnp.abs(pre)))) - 0.5
    ld_s[...] = -jnp.exp(log_w)
    eta = jax.nn.sigmoid(a0_ref[...] + _dot(x_wa, a2_ref[...]))
    gate = _dot(jax.nn.sigmoid(x_g), g2_ref[...])
    kk = x_k * kk_ref[...]
    kk_scale = jnp.minimum(lax.rsqrt(_head_reduce(kk * kk, ones_bd, two_pass=True)),
                           1.0 / KK_NORM_FLOOR)
    kf = x_k * (1.0 + (eta - 1.0) * ka_ref[...])
    bonus = _head_reduce(x_r * kf * rk_ref[...], ones_bd, two_pass=False) * x_v
    r_s[...] = x_r
    kf_s[...] = kf
    v_s[...] = x_v
    kn_s[...] = kk * kk_scale
    eta_s[...] = eta

    lane = lax.broadcasted_iota(jnp.int32, (RW_CHUNK, LANES), 1)
    first_head = lane < HEAD_DIM
    ri = lax.broadcasted_iota(jnp.int32, (LANES, LANES), 0)
    ci = lax.broadcasted_iota(jnp.int32, (LANES, LANES), 1)
    same_head = (ri < HEAD_DIM) == (ci < HEAD_DIM)
    strict_lower = same_head & (ci < ri)
    lower = same_head & (ci <= ri)
    eye = (ri == ci).astype(F32)
    tri = tri_ref[...]

    def stack_heads(t):
        return jnp.concatenate([jnp.where(first_head, t, 0.0), jnp.where(first_head, 0.0, t)],
                               axis=0)

    n_pairs = RW_WIDTH // LANES

    def chunk_operands(rows):
        ld = ld_s[rows, :]
        cum = _prefix_sum(tri, ld)
        total = cum[RW_CHUNK - 1:RW_CHUNK, :]
        e_incl = jnp.exp(cum)
        e_inv = jnp.exp(-cum)
        e_tail = jnp.exp(total - cum)
        w_total = jnp.exp(total)
        kn = kn_s[rows, :]
        kf_c = kf_s[rows, :]
        b = kn * eta_s[rows, :]
        a_dec = -kn * jnp.exp(cum - ld)
        r_dec = r_s[rows, :] * e_incl
        b_inv = b * e_inv
        k_inv = kf_c * e_inv
        b_tail = b * e_tail
        k_tail = kf_c * e_tail
        v_c = v_s[rows, :]
        units = []
        for p in range(n_pairs):
            cols = slice(p * LANES, (p + 1) * LANES)
            units.append(dict(
                a_sm=stack_heads(a_dec[:, cols]), r_sm=stack_heads(r_dec[:, cols]),
                v_sm=stack_heads(v_c[:, cols]), b_inv=b_inv[:, cols], k_inv=k_inv[:, cols],
                b_tail_t=stack_heads(b_tail[:, cols]).T, k_tail_t=stack_heads(k_tail[:, cols]).T,
                decay_diag=eye * w_total[:, cols]))
        return units

    def chunk_group(g, _):
        row_slices = [pl.ds(pl.multiple_of((g * RW_GROUP + c) * RW_CHUNK, RW_CHUNK), RW_CHUNK)
                      for c in range(RW_GROUP)]
        units = [u for rows in row_slices for u in chunk_operands(rows)]
        scores = [_dot_nt(jnp.concatenate([u["a_sm"], u["r_sm"]], axis=0),
                          jnp.concatenate([u["b_inv"], u["b_inv"], u["k_inv"], u["k_inv"]],
                                          axis=0)) for u in units]
        l_ab = [jnp.where(strict_lower, s[:LANES, :LANES], 0.0) for s in scores]
        l_ak = [jnp.where(strict_lower, s[:LANES, LANES:], 0.0) for s in scores]
        m_rb = [jnp.where(lower, s[LANES:, :LANES], 0.0) for s in scores]
        m_rk = [jnp.where(lower, s[LANES:, LANES:], 0.0) for s in scores]
        lakv = [_dot(l, u["v_sm"]) for l, u in zip(l_ak, units)]
        inv = [eye + l for l in l_ab]
        power = [_dot(l, l) for l in l_ab]
        for _ in range(4):
            both = [_dot(pw, jnp.concatenate([pw, iv], axis=1)) for pw, iv in zip(power, inv)]
            inv = [iv + x[:, LANES:] for iv, x in zip(inv, both)]
            power = [x[:, :LANES] for x in both]
        inv = [iv + _dot(pw, iv) for iv, pw in zip(inv, power)]
        pu = [_dot(iv, jnp.concatenate([u["a_sm"], lv], axis=1))
              for iv, u, lv in zip(inv, units, lakv)]
        zeros = jnp.zeros((LANES, LANES), F32)
        tail = [_dot(jnp.concatenate([jnp.concatenate([mb, mk], axis=1),
                                      jnp.concatenate([u["b_tail_t"], u["k_tail_t"]], axis=1)],
                                     axis=0),
                     jnp.concatenate([x, jnp.concatenate([zeros, u["v_sm"]], axis=1)], axis=0))
                for mb, mk, u, x in zip(m_rb, m_rk, units, pu)]
        states = [state_ref[p] for p in range(n_pairs)]
        for c, rows in enumerate(row_slices):
            new_states = []
            for p in range(n_pairs):
                i = c * n_pairs + p
                q_sm = units[i]["r_sm"] + tail[i][:LANES, :LANES]
                y_sm = _dot(q_sm, states[p]) + tail[i][:LANES, LANES:]
                g_bd = tail[i][LANES:, :LANES] + units[i]["decay_diag"]
                new_states.append(_dot(g_bd, states[p]) + tail[i][LANES:, LANES:])
                y_s[rows, p * LANES:(p + 1) * LANES] = y_sm[:RW_CHUNK, :] + y_sm[RW_CHUNK:, :]
            states = new_states
        for p in range(n_pairs):
            state_ref[p] = states[p]
        return 0

    lax.fori_loop(0, n_rows // (RW_CHUNK * RW_GROUP), chunk_group, 0)

    y = y_s[...]
    centered = y - _head_reduce(y, ones_bd, two_pass=False) * (1.0 / HEAD_DIM)
    var = _head_reduce(centered * centered, ones_bd, two_pass=False) * (1.0 / HEAD_DIM)
    yn = centered * lax.rsqrt(var + LNX_EPS) * lnw_ref[...] + lnb_ref[...]
    o_ref[...] = ((yn + bonus) * gate).astype(o_ref.dtype)


def _rwkv_call(rw_in, mu, w0, w2_pad, a0, a2_pad, g2, k_k, k_a, r_k, ln_w, ln_b, ones_bd, tri):
    b, s, width = rw_in.shape
    row_tile = next(t for t in (512, 256, 128) if s % t == 0)
    vec = lambda n: _const_spec((1, n))
    scratch_rows = pltpu.VMEM((row_tile, RW_WIDTH), F32)
    return pl.pallas_call(
        _rwkv_kernel,
        out_shape=jax.ShapeDtypeStruct((b, s, RW_WIDTH), BF16),
        grid=(b, s // row_tile),
        in_specs=[pl.BlockSpec((None, row_tile, width), lambda bi, j: (bi, j, 0)),
                  vec(width), vec(RW_WIDTH), _const_spec(w2_pad.shape), vec(RW_WIDTH),
                  _const_spec(a2_pad.shape), _const_spec(g2.shape), vec(RW_WIDTH),
                  vec(RW_WIDTH), vec(RW_WIDTH), vec(RW_WIDTH), vec(RW_WIDTH),
                  _const_spec(ones_bd.shape), _const_spec(tri.shape)],
        out_specs=pl.BlockSpec((None, row_tile, RW_WIDTH), lambda bi, j: (bi, j, 0)),
        scratch_shapes=[pltpu.VMEM((1, width), F32),
                        pltpu.VMEM((RW_WIDTH // LANES, LANES, LANES), F32)]
                       + [scratch_rows] * 7,
        compiler_params=pltpu.CompilerParams(
            dimension_semantics=("parallel", "arbitrary"), vmem_limit_bytes=VMEM_LIMIT),
        name="rwkv7",
    )(rw_in, mu, w0, w2_pad, a0, a2_pad, g2, k_k, k_a, r_k, ln_w, ln_b, ones_bd, tri)


def _head_block_diag(value, dtype):
    idx = jnp.arange(MXU_DIM) // HEAD_DIM
    return jnp.where(idx[:, None] == idx[None, :], value, 0.0).astype(dtype)


def kernel(x, norm_ffn1, ffn1_gate, ffn1_up, ffn1_down, norm_mix, w_in, sb_q_norm, sb_k_norm,
           sb_out_norm, rw_mu, rw_w0, rw_w2, rw_a0, rw_a2, rw_g2, rw_k_k, rw_k_a, rw_r_k,
           rw_ln_w, rw_ln_b, w_out, norm_ffn2, ffn2_gate, ffn2_up, ffn2_down):
    b, s, d = x.shape
    n = b * s
    heads = SB_WIDTH // HEAD_DIM
    mean_bd = _head_block_diag(1.0 / HEAD_DIM, BF16)
    ones_bd = _head_block_diag(1.0, BF16)
    tri = jnp.tril(jnp.ones((RW_CHUNK, RW_CHUNK), BF16))
    row = lambda t: t.reshape(1, -1)

    h = x.reshape(n, d)
    for l in range(norm_ffn1.shape[0]):
        h = _ffn_call(h, row(norm_ffn1[l]), ffn1_gate[l].astype(BF16), ffn1_up[l].astype(BF16),
                      ffn1_down[l].astype(BF16))

        w_in_l = w_in[l].astype(BF16)
        q, k, v_t, rw_in = _mix_in_call(
            h, row(norm_mix[l]), w_in_l, w_in_l[:, 2 * SB_WIDTH:3 * SB_WIDTH].T,
            row(jnp.tile(sb_q_norm[l], heads)), row(jnp.tile(sb_k_norm[l], heads)), mean_bd)

        o_sb = _sb_attn_call(q.reshape(b, s, SB_WIDTH), k.reshape(b, s, SB_WIDTH), v_t,
                             row(sb_out_norm[l]))

        zeros = jnp.zeros((DECAY_LORA, RW_WIDTH), F32)
        w2_pad = jnp.stack(_split_bf16(jnp.concatenate([rw_w2[l], zeros], axis=0)))
        a2_pad = jnp.concatenate([zeros, rw_a2[l]], axis=0).astype(BF16)
        o_rw = _rwkv_call(rw_in.reshape(b, s, RW_PROJ), row(rw_mu[l]), row(rw_w0[l]), w2_pad,
                          row(rw_a0[l]), a2_pad, rw_g2[l].astype(BF16), row(rw_k_k[l]),
                          row(rw_k_a[l]), row(rw_r_k[l]), row(rw_ln_w[l]), row(rw_ln_b[l]),
                          ones_bd, tri)

        h = _ffn_call(h, row(norm_ffn2[l]), ffn2_gate[l].astype(BF16), ffn2_up[l].astype(BF16),
                      ffn2_down[l].astype(BF16),
                      proj=(o_sb.reshape(n, SB_WIDTH), o_rw.reshape(n, RW_WIDTH),
                            w_out[l].astype(BF16)))
    return h.reshape(b, s, d)
```

```python
import functools

import jax
import jax.numpy as jnp
from jax import lax
from jax.experimental import pallas as pl
from jax.experimental.pallas import tpu as pltpu

F32 = jnp.float32
BF16 = jnp.bfloat16

HEAD_DIM = 64
SB_WIDTH = 512
RW_WIDTH = 512
DECAY_LORA = 64
AAA_LORA = 64
GATE_LORA = 128
RW_PROJ = 3 * RW_WIDTH + DECAY_LORA + AAA_LORA + GATE_LORA
RMS_EPS = 1e-6
LNX_EPS = 64e-5
KK_NORM_FLOOR = 1e-12

LANES = 128
MXU_DIM = 256
RW_CHUNK = 64
RW_GROUP = 4
SB_TILE = 256
SB_QTILE = 2 * SB_TILE
SB_HEADS_PER_STEP = 4
MASKED_LOG_BETA = -1e30
VMEM_LIMIT = 56 * 1024 * 1024


def _const_spec(shape):
    zeros = (0,) * len(shape)
    return pl.BlockSpec(shape, lambda *_: zeros, pipeline_mode=pl.Buffered(1))


def _dot(a, b):
    return jnp.dot(a.astype(BF16), b.astype(BF16), preferred_element_type=F32)


def _dot_nt(a, b):
    return lax.dot_general(a.astype(BF16), b.astype(BF16), (((1,), (1,)), ((), ())),
                           preferred_element_type=F32)


def _split_bf16(x):
    hi = x.astype(BF16)
    return hi, (x - hi.astype(F32)).astype(BF16)


def _dot_split(a, b_hi, b_lo):
    a_hi, a_lo = _split_bf16(a)
    return (jnp.dot(a_hi, b_hi, preferred_element_type=F32)
            + jnp.dot(a_hi, b_lo, preferred_element_type=F32)
            + jnp.dot(a_lo, b_hi, preferred_element_type=F32))


def _prefix_sum(tri, x):
    hi = x.astype(BF16)
    rest = x - hi.astype(F32)
    mid = rest.astype(BF16)
    lo = (rest - mid.astype(F32)).astype(BF16)
    return (jnp.dot(tri, hi, preferred_element_type=F32)
            + jnp.dot(tri, mid, preferred_element_type=F32)
            + jnp.dot(tri, lo, preferred_element_type=F32))


def _rms_norm_rows(x, gain):
    ms = jnp.mean(x * x, axis=-1, keepdims=True)
    return x * lax.rsqrt(ms + RMS_EPS) * gain


def _head_reduce(t, bd, two_pass):
    outs = []
    for half in range(t.shape[1] // MXU_DIM):
        th = t[:, half * MXU_DIM:(half + 1) * MXU_DIM]
        hi = th.astype(BF16)
        red = jnp.dot(hi, bd, preferred_element_type=F32)
        if two_pass:
            lo = (th - hi.astype(F32)).astype(BF16)
            red = red + jnp.dot(lo, bd, preferred_element_type=F32)
        outs.append(red)
    return jnp.concatenate(outs, axis=1)


def _swiglu_residual(x, gain_ref, wg_ref, wu_ref, wd_ref, ff_tile):
    h = _rms_norm_rows(x, gain_ref[...]).astype(BF16)
    acc = x
    for start in range(0, wg_ref.shape[1], ff_tile):
        cols = slice(start, min(start + ff_tile, wg_ref.shape[1]))
        gate = jnp.dot(h, wg_ref[:, cols], preferred_element_type=F32)
        up = jnp.dot(h, wu_ref[:, cols], preferred_element_type=F32)
        act = (gate * jax.nn.sigmoid(gate) * up * 0.5).astype(BF16)
        acc = acc + jnp.dot(act, wd_ref[cols, :], preferred_element_type=F32)
    return acc


def _ffn_kernel(x_ref, gain_ref, wg_ref, wu_ref, wd_ref, o_ref, *, ff_tile):
    o_ref[...] = _swiglu_residual(x_ref[...], gain_ref, wg_ref, wu_ref, wd_ref, ff_tile)


def _proj_ffn_kernel(x_ref, osb_ref, orw_ref, wo_ref, gain_ref, wg_ref, wu_ref, wd_ref,
                     o_ref, *, ff_tile):
    mixed = (jnp.dot(osb_ref[...], wo_ref[:SB_WIDTH, :], preferred_element_type=F32)
             + jnp.dot(orw_ref[...], wo_ref[SB_WIDTH:, :], preferred_element_type=F32))
    x = mixed + x_ref[...]
    o_ref[...] = _swiglu_residual(x, gain_ref, wg_ref, wu_ref, wd_ref, ff_tile)


def _ffn_tiles(n_tokens, d_ff):
    row_tile = 512 if n_tokens % 512 == 0 else n_tokens
    ff_tile = -(-d_ff // (2 * MXU_DIM)) * MXU_DIM if d_ff % MXU_DIM == 0 else d_ff
    return row_tile, ff_tile


def _ffn_call(x, gain, wg, wu, wd, proj=None):
    n, d = x.shape
    d_ff = wg.shape[1]
    row_tile, ff_tile = _ffn_tiles(n, d_ff)
    row_spec = pl.BlockSpec((row_tile, d), lambda i: (i, 0))
    weight_specs = [_const_spec((1, d)), _const_spec((d, d_ff)), _const_spec((d, d_ff)),
                    _const_spec((d_ff, d))]
    if proj is None:
        body = functools.partial(_ffn_kernel, ff_tile=ff_tile)
        in_specs = [row_spec] + weight_specs
        args = (x, gain, wg, wu, wd)
        name = "ffn1"
    else:
        o_sb, o_rw, w_out = proj
        body = functools.partial(_proj_ffn_kernel, ff_tile=ff_tile)
        in_specs = [row_spec,
                    pl.BlockSpec((row_tile, SB_WIDTH), lambda i: (i, 0)),
                    pl.BlockSpec((row_tile, RW_WIDTH), lambda i: (i, 0)),
                    _const_spec(w_out.shape)] + weight_specs
        args = (x, o_sb, o_rw, w_out, gain, wg, wu, wd)
        name = "ffn2"
    return pl.pallas_call(
        body,
        out_shape=jax.ShapeDtypeStruct((n, d), F32),
        grid=(n // row_tile,),
        in_specs=in_specs,
        out_specs=row_spec,
        compiler_params=pltpu.CompilerParams(
            dimension_semantics=("parallel",), vmem_limit_bytes=VMEM_LIMIT),
        name=name,
    )(*args)


def _mix_in_kernel(x_ref, gain_ref, w_ref, wvt_ref, qg_ref, kg_ref, mean_bd_ref,
                   q_ref, k_ref, vt_ref, rw_ref):
    h = _rms_norm_rows(x_ref[...], gain_ref[...]).astype(BF16)
    mean_bd = mean_bd_ref[...]

    def head_norm(t, gain):
        ms = _head_reduce(t * t, mean_bd, two_pass=False)
        return t * lax.rsqrt(ms + RMS_EPS) * gain

    q = jnp.dot(h, w_ref[:, 0:SB_WIDTH], preferred_element_type=F32)
    q_ref[...] = (head_norm(q, qg_ref[...]) * (HEAD_DIM ** -0.5)).astype(BF16)
    k = jnp.dot(h, w_ref[:, SB_WIDTH:2 * SB_WIDTH], preferred_element_type=F32)
    k_ref[...] = head_norm(k, kg_ref[...]).astype(BF16)
    vt = _dot_nt(wvt_ref[...], h).astype(BF16)
    for t in range(vt_ref.shape[0]):
        vt_ref[t] = vt[:, t * SB_TILE:(t + 1) * SB_TILE]
    rw_ref[...] = jnp.dot(h, w_ref[:, 3 * SB_WIDTH:], preferred_element_type=F32)


def _mix_in_call(x, gain, w_in, wv_t, q_gain, k_gain, mean_bd):
    n, d = x.shape
    row_tile = 2 * SB_TILE
    assert n % row_tile == 0
    row = lambda width: pl.BlockSpec((row_tile, width), lambda i: (i, 0))
    return pl.pallas_call(
        _mix_in_kernel,
        out_shape=(jax.ShapeDtypeStruct((n, SB_WIDTH), BF16),
                   jax.ShapeDtypeStruct((n, SB_WIDTH), BF16),
                   jax.ShapeDtypeStruct((n // SB_TILE, SB_WIDTH, SB_TILE), BF16),
                   jax.ShapeDtypeStruct((n, RW_PROJ), F32)),
        grid=(n // row_tile,),
        in_specs=[row(d), _const_spec((1, d)), _const_spec(w_in.shape),
                  _const_spec(wv_t.shape), _const_spec((1, SB_WIDTH)),
                  _const_spec((1, SB_WIDTH)), _const_spec(mean_bd.shape)],
        out_specs=(row(SB_WIDTH), row(SB_WIDTH),
                   pl.BlockSpec((row_tile // SB_TILE, SB_WIDTH, SB_TILE), lambda i: (i, 0, 0)),
                   row(RW_PROJ)),
        compiler_params=pltpu.CompilerParams(
            dimension_semantics=("parallel",), vmem_limit_bytes=VMEM_LIMIT),
        name="mix_in",
    )(x, gain, w_in, wv_t, q_gain, k_gain, mean_bd)


def _sb_attn_kernel(q_ref, k_ref, vt_ref, gain_ref, o_ref, *scratch):
    def query_tile(tile, _):
        rows = pl.ds(pl.multiple_of(tile * SB_QTILE, SB_QTILE), SB_QTILE)
        _sb_attn_tile(tile, q_ref.at[rows, :], k_ref, vt_ref, gain_ref, o_ref.at[rows, :],
                      *scratch)
        return 0

    lax.fori_loop(0, q_ref.shape[0] // SB_QTILE, query_tile, 0)


def _sb_attn_tile(tile, q_ref, k_ref, vt_ref, gain_ref, o_ref,
                  later_ref, acc_ref, z_ref, lb_ref, l1m_ref, row0_ref):
    n_tiles = 2 * tile + 2
    n_heads = q_ref.shape[1] // HEAD_DIM
    half = SB_QTILE // 2
    first_head = lax.broadcasted_iota(jnp.int32, (LANES, SB_QTILE), 0) < HEAD_DIM
    q_pairs = []
    for p in range(n_heads // 2):
        q_t = q_ref[:, p * LANES:(p + 1) * LANES].astype(F32).T
        q_pairs.append(jnp.concatenate([jnp.where(first_head, q_t, 0.0),
                                        jnp.where(first_head, 0.0, q_t)], axis=1).astype(BF16))
    kk = lax.broadcasted_iota(jnp.int32, (SB_TILE, SB_TILE), 0)
    jj = lax.broadcasted_iota(jnp.int32, (SB_TILE, SB_TILE), 1)
    later_keys = jnp.where(jj > kk, 1.0, 0.0).astype(BF16)

    def causal_mask(n_queries):
        key_pos = lax.broadcasted_iota(jnp.int32, (SB_TILE, n_heads * n_queries), 0)
        col = lax.broadcasted_iota(jnp.int32, (SB_TILE, n_heads * n_queries), 1)
        return key_pos < (col & (n_queries - 1))

    def key_tile(t):
        return jnp.maximum(n_tiles - 1 - t, 0)

    def scores(t, late_half=False):
        k_t = k_ref[pl.ds(pl.multiple_of(key_tile(t) * SB_TILE, SB_TILE), SB_TILE), :]
        parts = []
        for p, q_pair in enumerate(q_pairs):
            if late_half:
                q_pair = jnp.concatenate([q_pair[:, half:SB_QTILE], q_pair[:, SB_QTILE + half:]],
                                         axis=1)
            parts.append(jnp.dot(k_t[:, p * LANES:(p + 1) * LANES], q_pair,
                                 preferred_element_type=F32))
        return jnp.concatenate(parts, axis=1)

    def stage1(z, mask=None):
        neg_part = jnp.minimum(z, 0.0)
        neg_relu = neg_part - z
        soft = jnp.log(1.0 + jnp.exp(neg_part + neg_relu))
        log_beta = neg_part - soft
        log_1m_beta = neg_relu - soft
        if mask is not None:
            log_beta = jnp.where(mask, log_beta, MASKED_LOG_BETA)
            log_1m_beta = jnp.where(mask, log_1m_beta, 0.0)
        return log_beta, log_1m_beta.astype(BF16), log_1m_beta[0:1, :]

    def stage2(t, slot, sums):
        v_t = vt_ref[key_tile(t)]
        later = later_ref[...]
        att = jnp.exp(lb_ref[slot] + sums + later).astype(BF16)
        for h in range(n_heads):
            rows = slice(h * HEAD_DIM, (h + 1) * HEAD_DIM)
            acc_ref[rows, :] += jnp.dot(v_t[rows, :], att[:, h * SB_QTILE:(h + 1) * SB_QTILE],
                                        preferred_element_type=F32)
        later_ref[...] = later + sums[0:1, :] + row0_ref[slot]

    def trip(t, cur):
        other = 1 - cur
        z_next = scores(t + 1)
        sums = jnp.dot(later_keys, l1m_ref[other], preferred_element_type=F32)
        log_beta, log_1m_beta, key0_row = stage1(z_ref[cur])
        stage2(t - 1, other, sums)
        lb_ref[cur] = log_beta
        l1m_ref[cur] = log_1m_beta
        row0_ref[cur] = key0_row
        z_ref[other] = z_next

    later_ref[...] = jnp.zeros_like(later_ref)
    acc_ref[...] = jnp.zeros_like(acc_ref)

    z_t0 = scores(0, late_half=True)
    z_t1 = scores(1)
    z_t2 = scores(2)
    lb_t0, l1m_t0, row0_t0 = stage1(z_t0, causal_mask(half))
    sums_t0 = jnp.dot(later_keys, l1m_t0, preferred_element_type=F32)
    lb_ref[1], l1m_ref[1], row0_ref[1] = stage1(z_t1, causal_mask(SB_QTILE))
    att_t0 = jnp.exp(lb_t0 + sums_t0).astype(BF16)
    v_t0 = vt_ref[key_tile(0)]
    carry_t0 = sums_t0[0:1, :] + row0_t0
    for h in range(n_heads):
        rows = slice(h * HEAD_DIM, (h + 1) * HEAD_DIM)
        acc_ref[rows, half:] = jnp.dot(v_t0[rows, :], att_t0[:, h * half:(h + 1) * half],
                                       preferred_element_type=F32)
        later_ref[:, h * SB_QTILE + half:(h + 1) * SB_QTILE] = carry_t0[:, h * half:(h + 1) * half]
    z_ref[0] = z_t2

    def trip_pair(i, _):
        trip(2 * i + 2, 0)
        trip(2 * i + 3, 1)
        return 0

    lax.fori_loop(0, tile, trip_pair, 0)
    stage2(n_tiles - 1, 1, jnp.dot(later_keys, l1m_ref[1], preferred_element_type=F32))

    def head_norm(acc):
        ms = jnp.sum(acc * acc, axis=0, keepdims=True) * (1.0 / HEAD_DIM)
        return acc * lax.rsqrt(ms + RMS_EPS)

    out_t = jnp.concatenate([head_norm(acc_ref[h * HEAD_DIM:(h + 1) * HEAD_DIM, :])
                             for h in range(n_heads)], axis=0)
    o_ref[...] = (out_t.T * gain_ref[...]).astype(o_ref.dtype)


def _sb_attn_call(q, k, v_t, out_gain):
    b, s, width = q.shape
    step_lanes = SB_HEADS_PER_STEP * HEAD_DIM
    assert s % SB_QTILE == 0 and width % step_lanes == 0
    seq_spec = pl.BlockSpec((None, s, step_lanes), lambda bi, p: (bi, 0, p))
    all_heads = SB_HEADS_PER_STEP * SB_QTILE
    return pl.pallas_call(
        _sb_attn_kernel,
        out_shape=jax.ShapeDtypeStruct((b, s, width), BF16),
        grid=(b, width // step_lanes),
        in_specs=[seq_spec, seq_spec,
                  pl.BlockSpec((s // SB_TILE, step_lanes, SB_TILE), lambda bi, p: (bi, p, 0)),
                  pl.BlockSpec((1, step_lanes), lambda bi, p: (0, p))],
        out_specs=seq_spec,
        scratch_shapes=[pltpu.VMEM((1, all_heads), F32),
                        pltpu.VMEM((step_lanes, SB_QTILE), F32),
                        pltpu.VMEM((2, SB_TILE, all_heads), F32),
                        pltpu.VMEM((2, SB_TILE, all_heads), F32),
                        pltpu.VMEM((2, SB_TILE, all_heads), BF16),
                        pltpu.VMEM((2, 1, all_heads), F32)],
        compiler_params=pltpu.CompilerParams(
            dimension_semantics=("parallel", "parallel"), vmem_limit_bytes=VMEM_LIMIT),
        name="sb_attn",
    )(q, k, v_t, out_gain)


def _rwkv_kernel(u_ref, mu_ref, w0_ref, w2_ref, a0_ref, a2_ref, g2_ref, kk_ref, ka_ref,
                 rk_ref, lnw_ref, lnb_ref, ones_bd_ref, tri_ref, o_ref,
                 prev_ref, state_ref, r_s, kf_s, v_s, kn_s, eta_s, ld_s, y_s):
    n_rows = u_ref.shape[0]

    @pl.when(pl.program_id(1) == 0)
    def _():
        prev_ref[...] = jnp.zeros_like(prev_ref)
        state_ref[...] = jnp.zeros_like(state_ref)

    u = u_ref[...]
    row_id = lax.broadcasted_iota(jnp.int32, u.shape, 0)
    shifted = jnp.where(row_id == 0, prev_ref[...], pltpu.roll(u, 1, axis=0))
    prev_ref[...] = u[n_rows - 1:n_rows, :]
    u = u + (shifted - u) * mu_ref[...]

    x_r = u[:, 0:RW_WIDTH]
    x_k = u[:, RW_WIDTH:2 * RW_WIDTH]
    x_v = u[:, 2 * RW_WIDTH:3 * RW_WIDTH]
    x_wa = u[:, 3 * RW_WIDTH:3 * RW_WIDTH + DECAY_LORA + AAA_LORA]
    x_g = u[:, 3 * RW_WIDTH + DECAY_LORA + AAA_LORA:]
    ones_bd = ones_bd_ref[...]

    pre = -(w0_ref[...] + _dot_split(jnp.tanh(x_wa), w2_ref[0], w2_ref[1]))
    log_w = -(jnp.maximum(pre, 0.0) + jnp.log(1.0 + jnp.exp(-jnp.abs(pre)))) - 0.5
    ld_s[...] = -jnp.exp(log_w)
    eta = jax.nn.sigmoid(a0_ref[...] + _dot(x_wa, a2_ref[...]))
    gate = _dot(jax.nn.sigmoid(x_g), g2_ref[...])
    kk = x_k * kk_ref[...]
    kk_scale = jnp.minimum(lax.rsqrt(_head_reduce(kk * kk, ones_bd, two_pass=True)),
                           1.0 / KK_NORM_FLOOR)
    kf = x_k * (1.0 + (eta - 1.0) * ka_ref[...])
    bonus = _head_reduce(x_r * kf * rk_ref[...], ones_bd, two_pass=False) * x_v
    r_s[...] = x_r
    kf_s[...] = kf
    v_s[...] = x_v
    kn_s[...] = kk * kk_scale
    eta_s[...] = eta

    lane = lax.broadcasted_iota(jnp.int32, (RW_CHUNK, LANES), 1)
    first_head = lane < HEAD_DIM
    ri = lax.broadcasted_iota(jnp.int32, (LANES, LANES), 0)
    ci = lax.broadcasted_iota(jnp.int32, (LANES, LANES), 1)
    same_head = (ri < HEAD_DIM) == (ci < HEAD_DIM)
    strict_lower = same_head & (ci < ri)
    lower = same_head & (ci <= ri)
    eye = (ri == ci).astype(F32)
    tri = tri_ref[...]

    def stack_heads(t):
        return jnp.concatenate([jnp.where(first_head, t, 0.0), jnp.where(first_head, 0.0, t)],
                               axis=0)

    n_pairs = RW_WIDTH // LANES

    def chunk_operands(rows):
        ld = ld_s[rows, :]
        cum = _prefix_sum(tri, ld)
        total = cum[RW_CHUNK - 1:RW_CHUNK, :]
        e_incl = jnp.exp(cum)
        e_inv = jnp.exp(-cum)
        e_tail = jnp.exp(total - cum)
        w_total = jnp.exp(total)
        kn = kn_s[rows, :]
        kf_c = kf_s[rows, :]
        b = kn * eta_s[rows, :]
        a_dec = -kn * jnp.exp(cum - ld)
        r_dec = r_s[rows, :] * e_incl
        b_inv = b * e_inv
        k_inv = kf_c * e_inv
        b_tail = b * e_tail
        k_tail = kf_c * e_tail
        v_c = v_s[rows, :]
        units = []
        for p in range(n_pairs):
            cols = slice(p * LANES, (p + 1) * LANES)
            units.append(dict(
                a_sm=stack_heads(a_dec[:, cols]), r_sm=stack_heads(r_dec[:, cols]),
                v_sm=stack_heads(v_c[:, cols]), b_inv=b_inv[:, cols], k_inv=k_inv[:, cols],
                b_tail_t=stack_heads(b_tail[:, cols]).T, k_tail_t=stack_heads(k_tail[:, cols]).T,
                decay_diag=eye * w_total[:, cols]))
        return units

    def chunk_group(g, _):
        row_slices = [pl.ds(pl.multiple_of((g * RW_GROUP + c) * RW_CHUNK, RW_CHUNK), RW_CHUNK)
                      for c in range(RW_GROUP)]
        units = [u for rows in row_slices for u in chunk_operands(rows)]
        scores = [_dot_nt(jnp.concatenate([u["a_sm"], u["r_sm"]], axis=0),
                          jnp.concatenate([u["b_inv"], u["b_inv"], u["k_inv"], u["k_inv"]],
                                          axis=0)) for u in units]
        l_ab = [jnp.where(strict_lower, s[:LANES, :LANES], 0.0) for s in scores]
        l_ak = [jnp.where(strict_lower, s[:LANES, LANES:], 0.0) for s in scores]
        m_rb = [jnp.where(lower, s[LANES:, :LANES], 0.0) for s in scores]
        m_rk = [jnp.where(lower, s[LANES:, LANES:], 0.0) for s in scores]
        lakv = [_dot(l, u["v_sm"]) for l, u in zip(l_ak, units)]
        inv = [eye + l for l in l_ab]
        power = [_dot(l, l) for l in l_ab]
        for _ in range(4):
            both = [_dot(pw, jnp.concatenate([pw, iv], axis=1)) for pw, iv in zip(power, inv)]
            inv = [iv + x[:, LANES:] for iv, x in zip(inv, both)]
            power = [x[:, :LANES] for x in both]
        inv = [iv + _dot(pw, iv) for iv, pw in zip(inv, power)]
        pu = [_dot(iv, jnp.concatenate([u["a_sm"], lv], axis=1))
              for iv, u, lv in zip(inv, units, lakv)]
        zeros = jnp.zeros((LANES, LANES), F32)
        tail = [_dot(jnp.concatenate([jnp.concatenate([mb, mk], axis=1),
                                      jnp.concatenate([u["b_tail_t"], u["k_tail_t"]], axis=1)],
                                     axis=0),
                     jnp.concatenate([x, jnp.concatenate([zeros, u["v_sm"]], axis=1)], axis=0))
                for mb, mk, u, x in zip(m_rb, m_rk, units, pu)]
        states = [state_ref[p] for p in range(n_pairs)]
        for c, rows in enumerate(row_slices):
            new_states = []
            for p in range(n_pairs):
                i = c * n_pairs + p
                q_sm = units[i]["r_sm"] + tail[i][:LANES, :LANES]
                y_sm = _dot(q_sm, states[p]) + tail[i][:LANES, LANES:]
                g_bd = tail[i][LANES:, :LANES] + units[i]["decay_diag"]
                new_states.append(_dot(g_bd, states[p]) + tail[i][LANES:, LANES:])
                y_s[rows, p * LANES:(p + 1) * LANES] = y_sm[:RW_CHUNK, :] + y_sm[RW_CHUNK:, :]
            states = new_states
        for p in range(n_pairs):
            state_ref[p] = states[p]
        return 0

    lax.fori_loop(0, n_rows // (RW_CHUNK * RW_GROUP), chunk_group, 0)

    y = y_s[...]
    centered = y - _head_reduce(y, ones_bd, two_pass=False) * (1.0 / HEAD_DIM)
    var = _head_reduce(centered * centered, ones_bd, two_pass=False) * (1.0 / HEAD_DIM)
    yn = centered * lax.rsqrt(var + LNX_EPS) * lnw_ref[...] + lnb_ref[...]
    o_ref[...] = ((yn + bonus) * gate).astype(o_ref.dtype)


def _rwkv_call(rw_in, mu, w0, w2_pad, a0, a2_pad, g2, k_k, k_a, r_k, ln_w, ln_b, ones_bd, tri):
    b, s, width = rw_in.shape
    row_tile = next(t for t in (512, 256, 128) if s % t == 0)
    vec = lambda n: _const_spec((1, n))
    scratch_rows = pltpu.VMEM((row_tile, RW_WIDTH), F32)
    return pl.pallas_call(
        _rwkv_kernel,
        out_shape=jax.ShapeDtypeStruct((b, s, RW_WIDTH), BF16),
        grid=(b, s // row_tile),
        in_specs=[pl.BlockSpec((None, row_tile, width), lambda bi, j: (bi, j, 0)),
                  vec(width), vec(RW_WIDTH), _const_spec(w2_pad.shape), vec(RW_WIDTH),
                  _const_spec(a2_pad.shape), _const_spec(g2.shape), vec(RW_WIDTH),
                  vec(RW_WIDTH), vec(RW_WIDTH), vec(RW_WIDTH), vec(RW_WIDTH),
                  _const_spec(ones_bd.shape), _const_spec(tri.shape)],
        out_specs=pl.BlockSpec((None, row_tile, RW_WIDTH), lambda bi, j: (bi, j, 0)),
        scratch_shapes=[pltpu.VMEM((1, width), F32),
                        pltpu.VMEM((RW_WIDTH // LANES, LANES, LANES), F32)]
                       + [scratch_rows] * 7,
        compiler_params=pltpu.CompilerParams(
            dimension_semantics=("parallel", "arbitrary"), vmem_limit_bytes=VMEM_LIMIT),
        name="rwkv7",
    )(rw_in, mu, w0, w2_pad, a0, a2_pad, g2, k_k, k_a, r_k, ln_w, ln_b, ones_bd, tri)


def _head_block_diag(value, dtype):
    idx = jnp.arange(MXU_DIM) // HEAD_DIM
    return jnp.where(idx[:, None] == idx[None, :], value, 0.0).astype(dtype)


def kernel(x, norm_ffn1, ffn1_gate, ffn1_up, ffn1_down, norm_mix, w_in, sb_q_norm, sb_k_norm,
           sb_out_norm, rw_mu, rw_w0, rw_w2, rw_a0, rw_a2, rw_g2, rw_k_k, rw_k_a, rw_r_k,
           rw_ln_w, rw_ln_b, w_out, norm_ffn2, ffn2_gate, ffn2_up, ffn2_down):
    b, s, d = x.shape
    n = b * s
    heads = SB_WIDTH // HEAD_DIM
    mean_bd = _head_block_diag(1.0 / HEAD_DIM, BF16)
    ones_bd = _head_block_diag(1.0, BF16)
    tri = jnp.tril(jnp.ones((RW_CHUNK, RW_CHUNK), BF16))
    row = lambda t: t.reshape(1, -1)

    h = x.reshape(n, d)
    for l in range(norm_ffn1.shape[0]):
        h = _ffn_call(h, row(norm_ffn1[l]), ffn1_gate[l].astype(BF16), ffn1_up[l].astype(BF16),
                      ffn1_down[l].astype(BF16))

        w_in_l = w_in[l].astype(BF16)
        q, k, v_t, rw_in = _mix_in_call(
            h, row(norm_mix[l]), w_in_l, w_in_l[:, 2 * SB_WIDTH:3 * SB_WIDTH].T,
            row(jnp.tile(sb_q_norm[l], heads)), row(jnp.tile(sb_k_norm[l], heads)), mean_bd)

        o_sb = _sb_attn_call(q.reshape(b, s, SB_WIDTH), k.reshape(b, s, SB_WIDTH), v_t,
                             row(sb_out_norm[l]))

        zeros = jnp.zeros((DECAY_LORA, RW_WIDTH), F32)
        w2_pad = jnp.stack(_split_bf16(jnp.concatenate([rw_w2[l], zeros], axis=0)))
        a2_pad = jnp.concatenate([zeros, rw_a2[l]], axis=0).astype(BF16)
        o_rw = _rwkv_call(rw_in.reshape(b, s, RW_PROJ), row(rw_mu[l]), row(rw_w0[l]), w2_pad,
                          row(rw_a0[l]), a2_pad, rw_g2[l].astype(BF16), row(rw_k_k[l]),
                          row(rw_k_a[l]), row(rw_r_k[l]), row(rw_ln_w[l]), row(rw_ln_b[l]),
                          ones_bd, tri)

        h = _ffn_call(h, row(norm_ffn2[l]), ffn2_gate[l].astype(BF16), ffn2_up[l].astype(BF16),
                      ffn2_down[l].astype(BF16),
                      proj=(o_sb.reshape(n, SB_WIDTH), o_rw.reshape(n, RW_WIDTH),
                            w_out[l].astype(BF16)))
    return h.reshape(b, s, d)
```

```python
import functools

import jax
import jax.numpy as jnp
from jax import lax
from jax.experimental import pallas as pl
from jax.experimental.pallas import tpu as pltpu

F32 = jnp.float32
BF16 = jnp.bfloat16

HEAD_DIM = 64
SB_WIDTH = 512
RW_WIDTH = 512
DECAY_LORA = 64
AAA_LORA = 64
GATE_LORA = 128
RW_PROJ = 3 * RW_WIDTH + DECAY_LORA + AAA_LORA + GATE_LORA
RMS_EPS = 1e-6
LNX_EPS = 64e-5
KK_NORM_FLOOR = 1e-12

LANES = 128
MXU_DIM = 256
RW_CHUNK = 64
RW_GROUP = 4
SB_TILE = 256
SB_QTILE = 2 * SB_TILE
SB_HEADS_PER_STEP = 8
MASKED_LOG_BETA = -1e30
VMEM_LIMIT = 56 * 1024 * 1024


def _const_spec(shape):
    zeros = (0,) * len(shape)
    return pl.BlockSpec(shape, lambda *_: zeros, pipeline_mode=pl.Buffered(1))


def _dot(a, b):
    return jnp.dot(a.astype(BF16), b.astype(BF16), preferred_element_type=F32)


def _dot_nt(a, b):
    return lax.dot_general(a.astype(BF16), b.astype(BF16), (((1,), (1,)), ((), ())),
                           preferred_element_type=F32)


def _split_bf16(x):
    hi = x.astype(BF16)
    return hi, (x - hi.astype(F32)).astype(BF16)


def _dot_split(a, b_hi, b_lo):
    a_hi, a_lo = _split_bf16(a)
    return (jnp.dot(a_hi, b_hi, preferred_element_type=F32)
            + jnp.dot(a_hi, b_lo, preferred_element_type=F32)
            + jnp.dot(a_lo, b_hi, preferred_element_type=F32))


def _prefix_sum(tri, x):
    hi = x.astype(BF16)
    rest = x - hi.astype(F32)
    mid = rest.astype(BF16)
    lo = (rest - mid.astype(F32)).astype(BF16)
    return (jnp.dot(tri, hi, preferred_element_type=F32)
            + jnp.dot(tri, mid, preferred_element_type=F32)
            + jnp.dot(tri, lo, preferred_element_type=F32))


def _rms_norm_rows(x, gain):
    ms = jnp.mean(x * x, axis=-1, keepdims=True)
    return x * lax.rsqrt(ms + RMS_EPS) * gain


def _head_reduce(t, bd, two_pass):
    outs = []
    for half in range(t.shape[1] // MXU_DIM):
        th = t[:, half * MXU_DIM:(half + 1) * MXU_DIM]
        hi = th.astype(BF16)
        red = jnp.dot(hi, bd, preferred_element_type=F32)
        if two_pass:
            lo = (th - hi.astype(F32)).astype(BF16)
            red = red + jnp.dot(lo, bd, preferred_element_type=F32)
        outs.append(red)
    return jnp.concatenate(outs, axis=1)


def _swiglu_residual(x, gain_ref, wg_ref, wu_ref, wd_ref, ff_tile):
    h = _rms_norm_rows(x, gain_ref[...]).astype(BF16)
    acc = x
    for start in range(0, wg_ref.shape[1], ff_tile):
        cols = slice(start, min(start + ff_tile, wg_ref.shape[1]))
        gate = jnp.dot(h, wg_ref[:, cols], preferred_element_type=F32)
        up = jnp.dot(h, wu_ref[:, cols], preferred_element_type=F32)
        act = (gate * jax.nn.sigmoid(gate) * up * 0.5).astype(BF16)
        acc = acc + jnp.dot(act, wd_ref[cols, :], preferred_element_type=F32)
    return acc


def _ffn_kernel(x_ref, gain_ref, wg_ref, wu_ref, wd_ref, o_ref, *, ff_tile):
    o_ref[...] = _swiglu_residual(x_ref[...], gain_ref, wg_ref, wu_ref, wd_ref, ff_tile)


def _proj_ffn_kernel(x_ref, osb_ref, orw_ref, wo_ref, gain_ref, wg_ref, wu_ref, wd_ref,
                     o_ref, *, ff_tile):
    mixed = (jnp.dot(osb_ref[...], wo_ref[:SB_WIDTH, :], preferred_element_type=F32)
             + jnp.dot(orw_ref[...], wo_ref[SB_WIDTH:, :], preferred_element_type=F32))
    x = mixed + x_ref[...]
    o_ref[...] = _swiglu_residual(x, gain_ref, wg_ref, wu_ref, wd_ref, ff_tile)


def _ffn_tiles(n_tokens, d_ff):
    row_tile = 512 if n_tokens % 512 == 0 else n_tokens
    ff_tile = -(-d_ff // (2 * MXU_DIM)) * MXU_DIM if d_ff % MXU_DIM == 0 else d_ff
    return row_tile, ff_tile


def _ffn_call(x, gain, wg, wu, wd, proj=None):
    n, d = x.shape
    d_ff = wg.shape[1]
    row_tile, ff_tile = _ffn_tiles(n, d_ff)
    row_spec = pl.BlockSpec((row_tile, d), lambda i: (i, 0))
    weight_specs = [_const_spec((1, d)), _const_spec((d, d_ff)), _const_spec((d, d_ff)),
                    _const_spec((d_ff, d))]
    if proj is None:
        body = functools.partial(_ffn_kernel, ff_tile=ff_tile)
        in_specs = [row_spec] + weight_specs
        args = (x, gain, wg, wu, wd)
        name = "ffn1"
    else:
        o_sb, o_rw, w_out = proj
        body = functools.partial(_proj_ffn_kernel, ff_tile=ff_tile)
        in_specs = [row_spec,
                    pl.BlockSpec((row_tile, SB_WIDTH), lambda i: (i, 0)),
                    pl.BlockSpec((row_tile, RW_WIDTH), lambda i: (i, 0)),
                    _const_spec(w_out.shape)] + weight_specs
        args = (x, o_sb, o_rw, w_out, gain, wg, wu, wd)
        name = "ffn2"
    return pl.pallas_call(
        body,
        out_shape=jax.ShapeDtypeStruct((n, d), F32),
        grid=(n // row_tile,),
        in_specs=in_specs,
        out_specs=row_spec,
        compiler_params=pltpu.CompilerParams(
            dimension_semantics=("parallel",), vmem_limit_bytes=VMEM_LIMIT),
        name=name,
    )(*args)


def _mix_in_kernel(x_ref, gain_ref, w_ref, wvt_ref, qg_ref, kg_ref, mean_bd_ref,
                   q_ref, k_ref, vt_ref, rw_ref):
    h = _rms_norm_rows(x_ref[...], gain_ref[...]).astype(BF16)
    mean_bd = mean_bd_ref[...]

    def head_norm(t, gain):
        ms = _head_reduce(t * t, mean_bd, two_pass=False)
        return t * lax.rsqrt(ms + RMS_EPS) * gain

    q = jnp.dot(h, w_ref[:, 0:SB_WIDTH], preferred_element_type=F32)
    q_ref[...] = (head_norm(q, qg_ref[...]) * (HEAD_DIM ** -0.5)).astype(BF16)
    k = jnp.dot(h, w_ref[:, SB_WIDTH:2 * SB_WIDTH], preferred_element_type=F32)
    k_ref[...] = head_norm(k, kg_ref[...]).astype(BF16)
    vt = _dot_nt(wvt_ref[...], h).astype(BF16)
    for t in range(vt_ref.shape[0]):
        vt_ref[t] = vt[:, t * SB_TILE:(t + 1) * SB_TILE]
    rw_ref[...] = jnp.dot(h, w_ref[:, 3 * SB_WIDTH:], preferred_element_type=F32)


def _mix_in_call(x, gain, w_in, wv_t, q_gain, k_gain, mean_bd):
    n, d = x.shape
    row_tile = 2 * SB_TILE
    assert n % row_tile == 0
    row = lambda width: pl.BlockSpec((row_tile, width), lambda i: (i, 0))
    return pl.pallas_call(
        _mix_in_kernel,
        out_shape=(jax.ShapeDtypeStruct((n, SB_WIDTH), BF16),
                   jax.ShapeDtypeStruct((n, SB_WIDTH), BF16),
                   jax.ShapeDtypeStruct((n // SB_TILE, SB_WIDTH, SB_TILE), BF16),
                   jax.ShapeDtypeStruct((n, RW_PROJ), F32)),
        grid=(n // row_tile,),
        in_specs=[row(d), _const_spec((1, d)), _const_spec(w_in.shape),
                  _const_spec(wv_t.shape), _const_spec((1, SB_WIDTH)),
                  _const_spec((1, SB_WIDTH)), _const_spec(mean_bd.shape)],
        out_specs=(row(SB_WIDTH), row(SB_WIDTH),
                   pl.BlockSpec((row_tile // SB_TILE, SB_WIDTH, SB_TILE), lambda i: (i, 0, 0)),
                   row(RW_PROJ)),
        compiler_params=pltpu.CompilerParams(
            dimension_semantics=("parallel",), vmem_limit_bytes=VMEM_LIMIT),
        name="mix_in",
    )(x, gain, w_in, wv_t, q_gain, k_gain, mean_bd)


def _sb_attn_kernel(q_ref, k_ref, vt_ref, gain_ref, o_ref, *scratch):
    def query_tile(tile, _):
        rows = pl.ds(pl.multiple_of(tile * SB_QTILE, SB_QTILE), SB_QTILE)
        _sb_attn_tile(tile, q_ref.at[rows, :], k_ref, vt_ref, gain_ref, o_ref.at[rows, :],
                      *scratch)
        return 0

    lax.fori_loop(0, q_ref.shape[0] // SB_QTILE, query_tile, 0)


def _sb_attn_tile(tile, q_ref, k_ref, vt_ref, gain_ref, o_ref, later_ref, acc_ref):
    n_heads = q_ref.shape[1] // HEAD_DIM
    half = SB_QTILE // 2
    first_head = lax.broadcasted_iota(jnp.int32, (LANES, SB_QTILE), 0) < HEAD_DIM
    q_pairs = []
    for p in range(n_heads // 2):
        q_t = q_ref[:, p * LANES:(p + 1) * LANES].astype(F32).T
        q_pairs.append(jnp.concatenate([jnp.where(first_head, q_t, 0.0),
                                        jnp.where(first_head, 0.0, q_t)], axis=1).astype(BF16))
    kk = lax.broadcasted_iota(jnp.int32, (SB_TILE, SB_TILE), 0)
    jj = lax.broadcasted_iota(jnp.int32, (SB_TILE, SB_TILE), 1)
    later_keys = jnp.where(jj > kk, 1.0, 0.0).astype(BF16)

    def key_tile(j, q_start, n_queries, masked):
        k_t = k_ref[pl.ds(pl.multiple_of(j * SB_TILE, SB_TILE), SB_TILE), :]
        v_t = vt_ref[j]
        q_cols = slice(q_start, q_start + n_queries)
        if masked:
            causal = (lax.broadcasted_iota(jnp.int32, (SB_TILE, n_queries), 0)
                      < lax.broadcasted_iota(jnp.int32, (SB_TILE, n_queries), 1))

        def logits(p):
            q_pair = jnp.concatenate([q_pairs[p][:, q_cols],
                                      q_pairs[p][:, SB_QTILE + q_start:SB_QTILE + q_start + n_queries]],
                                     axis=1)
            return jnp.dot(k_t[:, p * LANES:(p + 1) * LANES], q_pair, preferred_element_type=F32)

        def finish(h, log_beta, sums, key0_row):
            rows = slice(h * HEAD_DIM, (h + 1) * HEAD_DIM)
            cols = slice(h * SB_QTILE + q_start, h * SB_QTILE + q_start + n_queries)
            later = later_ref[:, cols]
            att = jnp.exp(log_beta + sums + later).astype(BF16)
            acc_ref[rows, q_cols] += jnp.dot(v_t[rows, :], att, preferred_element_type=F32)
            later_ref[:, cols] = later + sums[0:1, :] + key0_row

        z_pair = logits(0)
        pending = None
        for h in range(n_heads):
            if h % 2 == 0:
                z_now = z_pair
                if h + 2 < n_heads:
                    z_pair = logits(h // 2 + 1)
            z = z_now[:, (h % 2) * n_queries:(h % 2 + 1) * n_queries]
            neg_part = jnp.minimum(z, 0.0)
            neg_relu = neg_part - z
            soft = jnp.log(1.0 + jnp.exp(neg_part + neg_relu))
            log_beta = neg_part - soft
            log_1m_beta = neg_relu - soft
            if masked:
                log_beta = jnp.where(causal, log_beta, MASKED_LOG_BETA)
                log_1m_beta = jnp.where(causal, log_1m_beta, 0.0)
            sums = jnp.dot(later_keys, log_1m_beta.astype(BF16), preferred_element_type=F32)
            if pending is not None:
                finish(*pending)
            pending = (h, log_beta, sums, log_1m_beta[0:1, :])
        finish(*pending)

    later_ref[...] = jnp.zeros_like(later_ref)
    acc_ref[...] = jnp.zeros_like(acc_ref)
    key_tile(2 * tile + 1, half, half, masked=True)
    key_tile(2 * tile, 0, SB_QTILE, masked=True)

    def earlier_tile(i, _):
        key_tile(2 * tile - 1 - i, 0, SB_QTILE, masked=False)
        return 0

    lax.fori_loop(0, 2 * tile, earlier_tile, 0)

    def head_norm(acc):
        ms = jnp.sum(acc * acc, axis=0, keepdims=True) * (1.0 / HEAD_DIM)
        return acc * lax.rsqrt(ms + RMS_EPS)

    out_t = jnp.concatenate([head_norm(acc_ref[h * HEAD_DIM:(h + 1) * HEAD_DIM, :])
                             for h in range(n_heads)], axis=0)
    o_ref[...] = (out_t.T * gain_ref[...]).astype(o_ref.dtype)


def _sb_attn_call(q, k, v_t, out_gain):
    b, s, width = q.shape
    step_lanes = SB_HEADS_PER_STEP * HEAD_DIM
    assert s % SB_QTILE == 0 and width % step_lanes == 0
    seq_spec = pl.BlockSpec((None, s, step_lanes), lambda bi, p: (bi, 0, p))
    all_heads = SB_HEADS_PER_STEP * SB_QTILE
    return pl.pallas_call(
        _sb_attn_kernel,
        out_shape=jax.ShapeDtypeStruct((b, s, width), BF16),
        grid=(b, width // step_lanes),
        in_specs=[seq_spec, seq_spec,
                  pl.BlockSpec((s // SB_TILE, step_lanes, SB_TILE), lambda bi, p: (bi, p, 0)),
                  pl.BlockSpec((1, step_lanes), lambda bi, p: (0, p))],
        out_specs=seq_spec,
        scratch_shapes=[pltpu.VMEM((1, all_heads), F32),
                        pltpu.VMEM((step_lanes, SB_QTILE), F32)],
        compiler_params=pltpu.CompilerParams(
            dimension_semantics=("parallel", "parallel"), vmem_limit_bytes=VMEM_LIMIT),
        name="sb_attn",
    )(q, k, v_t, out_gain)


def _rwkv_kernel(u_ref, mu_ref, w0_ref, w2_ref, a0_ref, a2_ref, g2_ref, kk_ref, ka_ref,
                 rk_ref, lnw_ref, lnb_ref, ones_bd_ref, tri_ref, o_ref,
                 prev_ref, state_ref, r_s, kf_s, v_s, kn_s, eta_s, ld_s, y_s):
    n_rows = u_ref.shape[0]

    @pl.when(pl.program_id(1) == 0)
    def _():
        prev_ref[...] = jnp.zeros_like(prev_ref)
        state_ref[...] = jnp.zeros_like(state_ref)

    u = u_ref[...]
    row_id = lax.broadcasted_iota(jnp.int32, u.shape, 0)
    shifted = jnp.where(row_id == 0, prev_ref[...], pltpu.roll(u, 1, axis=0))
    prev_ref[...] = u[n_rows - 1:n_rows, :]
    u = u + (shifted - u) * mu_ref[...]

    x_r = u[:, 0:RW_WIDTH]
    x_k = u[:, RW_WIDTH:2 * RW_WIDTH]
    x_v = u[:, 2 * RW_WIDTH:3 * RW_WIDTH]
    x_wa = u[:, 3 * RW_WIDTH:3 * RW_WIDTH + DECAY_LORA + AAA_LORA]
    x_g = u[:, 3 * RW_WIDTH + DECAY_LORA + AAA_LORA:]
    ones_bd = ones_bd_ref[...]

    pre = -(w0_ref[...] + _dot_split(jnp.tanh(x_wa), w2_ref[0], w2_ref[1]))
    log_w = -(jnp.maximum(pre, 0.0) + jnp.log(1.0 + jnp.exp(-jnp.abs(pre)))) - 0.5
    ld_s[...] = -jnp.exp(log_w)
    eta = jax.nn.sigmoid(a0_ref[...] + _dot(x_wa, a2_ref[...]))
    gate = _dot(jax.nn.sigmoid(x_g), g2_ref[...])
    kk = x_k * kk_ref[...]
    kk_scale = jnp.minimum(lax.rsqrt(_head_reduce(kk * kk, ones_bd, two_pass=True)),
                           1.0 / KK_NORM_FLOOR)
    kf = x_k * (1.0 + (eta - 1.0) * ka_ref[...])
    bonus = _head_reduce(x_r * kf * rk_ref[...], ones_bd, two_pass=False) * x_v
    r_s[...] = x_r
    kf_s[...] = kf
    v_s[...] = x_v
    kn_s[...] = kk * kk_scale
    eta_s[...] = eta

    lane = lax.broadcasted_iota(jnp.int32, (RW_CHUNK, LANES), 1)
    first_head = lane < HEAD_DIM
    ri = lax.broadcasted_iota(jnp.int32, (LANES, LANES), 0)
    ci = lax.broadcasted_iota(jnp.int32, (LANES, LANES), 1)
    same_head = (ri < HEAD_DIM) == (ci < HEAD_DIM)
    strict_lower = same_head & (ci < ri)
    lower = same_head & (ci <= ri)
    eye = (ri == ci).astype(F32)
    tri = tri_ref[...]

    def stack_heads(t):
        return jnp.concatenate([jnp.where(first_head, t, 0.0), jnp.where(first_head, 0.0, t)],
                               axis=0)

    n_pairs = RW_WIDTH // LANES

    def chunk_operands(rows):
        ld = ld_s[rows, :]
        cum = _prefix_sum(tri, ld)
        total = cum[RW_CHUNK - 1:RW_CHUNK, :]
        e_incl = jnp.exp(cum)
        e_inv = jnp.exp(-cum)
        e_tail = jnp.exp(total - cum)
        w_total = jnp.exp(total)
        kn = kn_s[rows, :]
        kf_c = kf_s[rows, :]
        b = kn * eta_s[rows, :]
        a_dec = -kn * jnp.exp(cum - ld)
        r_dec = r_s[rows, :] * e_incl
        b_inv = b * e_inv
        k_inv = kf_c * e_inv
        b_tail = b * e_tail
        k_tail = kf_c * e_tail
        v_c = v_s[rows, :]
        units = []
        for p in range(n_pairs):
            cols = slice(p * LANES, (p + 1) * LANES)
            units.append(dict(
                a_sm=stack_heads(a_dec[:, cols]), r_sm=stack_heads(r_dec[:, cols]),
                v_sm=stack_heads(v_c[:, cols]), b_inv=b_inv[:, cols], k_inv=k_inv[:, cols],
                b_tail_t=stack_heads(b_tail[:, cols]).T, k_tail_t=stack_heads(k_tail[:, cols]).T,
                decay_diag=eye * w_total[:, cols]))
        return units

    def chunk_group(g, _):
        row_slices = [pl.ds(pl.multiple_of((g * RW_GROUP + c) * RW_CHUNK, RW_CHUNK), RW_CHUNK)
                      for c in range(RW_GROUP)]
        units = [u for rows in row_slices for u in chunk_operands(rows)]
        scores = [_dot_nt(jnp.concatenate([u["a_sm"], u["r_sm"]], axis=0),
                          jnp.concatenate([u["b_inv"], u["b_inv"], u["k_inv"], u["k_inv"]],
                                          axis=0)) for u in units]
        l_ab = [jnp.where(strict_lower, s[:LANES, :LANES], 0.0) for s in scores]
        l_ak = [jnp.where(strict_lower, s[:LANES, LANES:], 0.0) for s in scores]
        m_rb = [jnp.where(lower, s[LANES:, :LANES], 0.0) for s in scores]
        m_rk = [jnp.where(lower, s[LANES:, LANES:], 0.0) for s in scores]
        lakv = [_dot(l, u["v_sm"]) for l, u in zip(l_ak, units)]
        inv = [eye + l for l in l_ab]
        power = [_dot(l, l) for l in l_ab]
        for _ in range(4):
            both = [_dot(pw, jnp.concatenate([pw, iv], axis=1)) for pw, iv in zip(power, inv)]
            inv = [iv + x[:, LANES:] for iv, x in zip(inv, both)]
            power = [x[:, :LANES] for x in both]
        inv = [iv + _dot(pw, iv) for iv, pw in zip(inv, power)]
        pu = [_dot(iv, jnp.concatenate([u["a_sm"], lv], axis=1))
              for iv, u, lv in zip(inv, units, lakv)]
        zeros = jnp.zeros((LANES, LANES), F32)
        tail = [_dot(jnp.concatenate([jnp.concatenate([mb, mk], axis=1),
                                      jnp.concatenate([u["b_tail_t"], u["k_tail_t"]], axis=1)],
                                     axis=0),
                     jnp.concatenate([x, jnp.concatenate([zeros, u["v_sm"]], axis=1)], axis=0))
                for mb, mk, u, x in zip(m_rb, m_rk, units, pu)]
        states = [state_ref[p] for p in range(n_pairs)]
        for c, rows in enumerate(row_slices):
            new_states = []
            for p in range(n_pairs):
                i = c * n_pairs + p
                q_sm = units[i]["r_sm"] + tail[i][:LANES, :LANES]
                y_sm = _dot(q_sm, states[p]) + tail[i][:LANES, LANES:]
                g_bd = tail[i][LANES:, :LANES] + units[i]["decay_diag"]
                new_states.append(_dot(g_bd, states[p]) + tail[i][LANES:, LANES:])
                y_s[rows, p * LANES:(p + 1) * LANES] = y_sm[:RW_CHUNK, :] + y_sm[RW_CHUNK:, :]
            states = new_states
        for p in range(n_pairs):
            state_ref[p] = states[p]
        return 0

    lax.fori_loop(0, n_rows // (RW_CHUNK * RW_GROUP), chunk_group, 0)

    y = y_s[...]
    centered = y - _head_reduce(y, ones_bd, two_pass=False) * (1.0 / HEAD_DIM)
    var = _head_reduce(centered * centered, ones_bd, two_pass=False) * (1.0 / HEAD_DIM)
    yn = centered * lax.rsqrt(var + LNX_EPS) * lnw_ref[...] + lnb_ref[...]
    o_ref[...] = ((yn + bonus) * gate).astype(o_ref.dtype)


def _rwkv_call(rw_in, mu, w0, w2_pad, a0, a2_pad, g2, k_k, k_a, r_k, ln_w, ln_b, ones_bd, tri):
    b, s, width = rw_in.shape
    row_tile = next(t for t in (512, 256, 128) if s % t == 0)
    vec = lambda n: _const_spec((1, n))
    scratch_rows = pltpu.VMEM((row_tile, RW_WIDTH), F32)
    return pl.pallas_call(
        _rwkv_kernel,
        out_shape=jax.ShapeDtypeStruct((b, s, RW_WIDTH), BF16),
        grid=(b, s // row_tile),
        in_specs=[pl.BlockSpec((None, row_tile, width), lambda bi, j: (bi, j, 0)),
                  vec(width), vec(RW_WIDTH), _const_spec(w2_pad.shape), vec(RW_WIDTH),
                  _const_spec(a2_pad.shape), _const_spec(g2.shape), vec(RW_WIDTH),
                  vec(RW_WIDTH), vec(RW_WIDTH), vec(RW_WIDTH), vec(RW_WIDTH),
                  _const_spec(ones_bd.shape), _const_spec(tri.shape)],
        out_specs=pl.BlockSpec((None, row_tile, RW_WIDTH), lambda bi, j: (bi, j, 0)),
        scratch_shapes=[pltpu.VMEM((1, width), F32),
                        pltpu.VMEM((RW_WIDTH // LANES, LANES, LANES), F32)]
                       + [scratch_rows] * 7,
        compiler_params=pltpu.CompilerParams(
            dimension_semantics=("parallel", "arbitrary"), vmem_limit_bytes=VMEM_LIMIT),
        name="rwkv7",
    )(rw_in, mu, w0, w2_pad, a0, a2_pad, g2, k_k, k_a, r_k, ln_w, ln_b, ones_bd, tri)


def _head_block_diag(value, dtype):
    idx = jnp.arange(MXU_DIM) // HEAD_DIM
    return jnp.where(idx[:, None] == idx[None, :], value, 0.0).astype(dtype)


def kernel(x, norm_ffn1, ffn1_gate, ffn1_up, ffn1_down, norm_mix, w_in, sb_q_norm, sb_k_norm,
           sb_out_norm, rw_mu, rw_w0, rw_w2, rw_a0, rw_a2, rw_g2, rw_k_k, rw_k_a, rw_r_k,
           rw_ln_w, rw_ln_b, w_out, norm_ffn2, ffn2_gate, ffn2_up, ffn2_down):
    b, s, d = x.shape
    n = b * s
    heads = SB_WIDTH // HEAD_DIM
    mean_bd = _head_block_diag(1.0 / HEAD_DIM, BF16)
    ones_bd = _head_block_diag(1.0, BF16)
    tri = jnp.tril(jnp.ones((RW_CHUNK, RW_CHUNK), BF16))
    row = lambda t: t.reshape(1, -1)

    h = x.reshape(n, d)
    for l in range(norm_ffn1.shape[0]):
        h = _ffn_call(h, row(norm_ffn1[l]), ffn1_gate[l].astype(BF16), ffn1_up[l].astype(BF16),
                      ffn1_down[l].astype(BF16))

        w_in_l = w_in[l].astype(BF16)
        q, k, v_t, rw_in = _mix_in_call(
            h, row(norm_mix[l]), w_in_l, w_in_l[:, 2 * SB_WIDTH:3 * SB_WIDTH].T,
            row(jnp.tile(sb_q_norm[l], heads)), row(jnp.tile(sb_k_norm[l], heads)), mean_bd)

        o_sb = _sb_attn_call(q.reshape(b, s, SB_WIDTH), k.reshape(b, s, SB_WIDTH), v_t,
                             row(sb_out_norm[l]))

        zeros = jnp.zeros((DECAY_LORA, RW_WIDTH), F32)
        w2_pad = jnp.stack(_split_bf16(jnp.concatenate([rw_w2[l], zeros], axis=0)))
        a2_pad = jnp.concatenate([zeros, rw_a2[l]], axis=0).astype(BF16)
        o_rw = _rwkv_call(rw_in.reshape(b, s, RW_PROJ), row(rw_mu[l]), row(rw_w0[l]), w2_pad,
                          row(rw_a0[l]), a2_pad, rw_g2[l].astype(BF16), row(rw_k_k[l]),
                          row(rw_k_a[l]), row(rw_r_k[l]), row(rw_ln_w[l]), row(rw_ln_b[l]),
                          ones_bd, tri)

        h = _ffn_call(h, row(norm_ffn2[l]), ffn2_gate[l].astype(BF16), ffn2_up[l].astype(BF16),
                      ffn2_down[l].astype(BF16),
                      proj=(o_sb.reshape(n, SB_WIDTH), o_rw.reshape(n, RW_WIDTH),
                            w_out[l].astype(BF16)))
    return h.reshape(b, s, d)
```

```python
import functools

import jax
import jax.numpy as jnp
from jax import lax
from jax.experimental import pallas as pl
from jax.experimental.pallas import tpu as pltpu

F32 = jnp.float32
BF16 = jnp.bfloat16

HEAD_DIM = 64
SB_WIDTH = 512
RW_WIDTH = 512
DECAY_LORA = 64
AAA_LORA = 64
GATE_LORA = 128
RW_PROJ = 3 * RW_WIDTH + DECAY_LORA + AAA_LORA + GATE_LORA
RMS_EPS = 1e-6
LNX_EPS = 64e-5
KK_NORM_FLOOR = 1e-12

LANES = 128
MXU_DIM = 256
RW_CHUNK = 64
RW_GROUP = 4
SB_TILE = 256
SB_QTILE = 2 * SB_TILE
SB_HEADS_PER_STEP = 8
MASKED_LOG_BETA = -1e30
VMEM_LIMIT = 56 * 1024 * 1024


def _const_spec(shape):
    zeros = (0,) * len(shape)
    return pl.BlockSpec(shape, lambda *_: zeros, pipeline_mode=pl.Buffered(1))


def _dot(a, b):
    return jnp.dot(a.astype(BF16), b.astype(BF16), preferred_element_type=F32)


def _dot_nt(a, b):
    return lax.dot_general(a.astype(BF16), b.astype(BF16), (((1,), (1,)), ((), ())),
                           preferred_element_type=F32)


def _split_bf16(x):
    hi = x.astype(BF16)
    return hi, (x - hi.astype(F32)).astype(BF16)


def _dot_split(a, b_hi, b_lo):
    a_hi, a_lo = _split_bf16(a)
    return (jnp.dot(a_hi, b_hi, preferred_element_type=F32)
            + jnp.dot(a_hi, b_lo, preferred_element_type=F32)
            + jnp.dot(a_lo, b_hi, preferred_element_type=F32))


def _prefix_sum(tri, x):
    hi = x.astype(BF16)
    rest = x - hi.astype(F32)
    mid = rest.astype(BF16)
    lo = (rest - mid.astype(F32)).astype(BF16)
    return (jnp.dot(tri, hi, preferred_element_type=F32)
            + jnp.dot(tri, mid, preferred_element_type=F32)
            + jnp.dot(tri, lo, preferred_element_type=F32))


def _rms_norm_rows(x, gain):
    ms = jnp.mean(x * x, axis=-1, keepdims=True)
    return x * lax.rsqrt(ms + RMS_EPS) * gain


def _head_reduce(t, bd, two_pass):
    outs = []
    for half in range(t.shape[1] // MXU_DIM):
        th = t[:, half * MXU_DIM:(half + 1) * MXU_DIM]
        hi = th.astype(BF16)
        red = jnp.dot(hi, bd, preferred_element_type=F32)
        if two_pass:
            lo = (th - hi.astype(F32)).astype(BF16)
            red = red + jnp.dot(lo, bd, preferred_element_type=F32)
        outs.append(red)
    return jnp.concatenate(outs, axis=1)


def _swiglu_residual(x, gain_ref, wg_ref, wu_ref, wd_ref, ff_tile):
    h = _rms_norm_rows(x, gain_ref[...]).astype(BF16)
    acc = x
    for start in range(0, wg_ref.shape[1], ff_tile):
        cols = slice(start, min(start + ff_tile, wg_ref.shape[1]))
        gate = jnp.dot(h, wg_ref[:, cols], preferred_element_type=F32)
        up = jnp.dot(h, wu_ref[:, cols], preferred_element_type=F32)
        act = (gate * jax.nn.sigmoid(gate) * up * 0.5).astype(BF16)
        acc = acc + jnp.dot(act, wd_ref[cols, :], preferred_element_type=F32)
    return acc


def _ffn_kernel(x_ref, gain_ref, wg_ref, wu_ref, wd_ref, o_ref, *, ff_tile):
    o_ref[...] = _swiglu_residual(x_ref[...], gain_ref, wg_ref, wu_ref, wd_ref, ff_tile)


def _proj_ffn_kernel(x_ref, osb_ref, orw_ref, wo_ref, gain_ref, wg_ref, wu_ref, wd_ref,
                     o_ref, *, ff_tile):
    mixed = (jnp.dot(osb_ref[...], wo_ref[:SB_WIDTH, :], preferred_element_type=F32)
             + jnp.dot(orw_ref[...], wo_ref[SB_WIDTH:, :], preferred_element_type=F32))
    x = mixed + x_ref[...]
    o_ref[...] = _swiglu_residual(x, gain_ref, wg_ref, wu_ref, wd_ref, ff_tile)


def _ffn_tiles(n_tokens, d_ff):
    row_tile = 512 if n_tokens % 512 == 0 else n_tokens
    ff_tile = -(-d_ff // (2 * MXU_DIM)) * MXU_DIM if d_ff % MXU_DIM == 0 else d_ff
    return row_tile, ff_tile


def _ffn_call(x, gain, wg, wu, wd, proj=None):
    n, d = x.shape
    d_ff = wg.shape[1]
    row_tile, ff_tile = _ffn_tiles(n, d_ff)
    row_spec = pl.BlockSpec((row_tile, d), lambda i: (i, 0))
    weight_specs = [_const_spec((1, d)), _const_spec((d, d_ff)), _const_spec((d, d_ff)),
                    _const_spec((d_ff, d))]
    if proj is None:
        body = functools.partial(_ffn_kernel, ff_tile=ff_tile)
        in_specs = [row_spec] + weight_specs
        args = (x, gain, wg, wu, wd)
        name = "ffn1"
    else:
        o_sb, o_rw, w_out = proj
        body = functools.partial(_proj_ffn_kernel, ff_tile=ff_tile)
        in_specs = [row_spec,
                    pl.BlockSpec((row_tile, SB_WIDTH), lambda i: (i, 0)),
                    pl.BlockSpec((row_tile, RW_WIDTH), lambda i: (i, 0)),
                    _const_spec(w_out.shape)] + weight_specs
        args = (x, o_sb, o_rw, w_out, gain, wg, wu, wd)
        name = "ffn2"
    return pl.pallas_call(
        body,
        out_shape=jax.ShapeDtypeStruct((n, d), F32),
        grid=(n // row_tile,),
        in_specs=in_specs,
        out_specs=row_spec,
        compiler_params=pltpu.CompilerParams(
            dimension_semantics=("parallel",), vmem_limit_bytes=VMEM_LIMIT),
        name=name,
    )(*args)


def _mix_in_kernel(x_ref, gain_ref, w_ref, wvt_ref, qg_ref, kg_ref, mean_bd_ref,
                   q_ref, k_ref, vt_ref, rw_ref):
    h = _rms_norm_rows(x_ref[...], gain_ref[...]).astype(BF16)
    mean_bd = mean_bd_ref[...]

    def head_norm(t, gain):
        ms = _head_reduce(t * t, mean_bd, two_pass=False)
        return t * lax.rsqrt(ms + RMS_EPS) * gain

    q = jnp.dot(h, w_ref[:, 0:SB_WIDTH], preferred_element_type=F32)
    q_ref[...] = (head_norm(q, qg_ref[...]) * (HEAD_DIM ** -0.5)).astype(BF16)
    k = jnp.dot(h, w_ref[:, SB_WIDTH:2 * SB_WIDTH], preferred_element_type=F32)
    k_ref[...] = head_norm(k, kg_ref[...]).astype(BF16)
    vt = _dot_nt(wvt_ref[...], h).astype(BF16)
    for t in range(vt_ref.shape[0]):
        vt_ref[t] = vt[:, t * SB_TILE:(t + 1) * SB_TILE]
    rw_ref[...] = jnp.dot(h, w_ref[:, 3 * SB_WIDTH:], preferred_element_type=F32)


def _mix_in_call(x, gain, w_in, wv_t, q_gain, k_gain, mean_bd):
    n, d = x.shape
    row_tile = 2 * SB_TILE
    assert n % row_tile == 0
    row = lambda width: pl.BlockSpec((row_tile, width), lambda i: (i, 0))
    return pl.pallas_call(
        _mix_in_kernel,
        out_shape=(jax.ShapeDtypeStruct((n, SB_WIDTH), BF16),
                   jax.ShapeDtypeStruct((n, SB_WIDTH), BF16),
                   jax.ShapeDtypeStruct((n // SB_TILE, SB_WIDTH, SB_TILE), BF16),
                   jax.ShapeDtypeStruct((n, RW_PROJ), F32)),
        grid=(n // row_tile,),
        in_specs=[row(d), _const_spec((1, d)), _const_spec(w_in.shape),
                  _const_spec(wv_t.shape), _const_spec((1, SB_WIDTH)),
                  _const_spec((1, SB_WIDTH)), _const_spec(mean_bd.shape)],
        out_specs=(row(SB_WIDTH), row(SB_WIDTH),
                   pl.BlockSpec((row_tile // SB_TILE, SB_WIDTH, SB_TILE), lambda i: (i, 0, 0)),
                   row(RW_PROJ)),
        compiler_params=pltpu.CompilerParams(
            dimension_semantics=("parallel",), vmem_limit_bytes=VMEM_LIMIT),
        name="mix_in",
    )(x, gain, w_in, wv_t, q_gain, k_gain, mean_bd)


def _sb_attn_kernel(q_ref, k_ref, vt_ref, gain_ref, o_ref, *scratch):
    def query_tile(tile, _):
        rows = pl.ds(pl.multiple_of(tile * SB_QTILE, SB_QTILE), SB_QTILE)
        _sb_attn_tile(tile, q_ref.at[rows, :], k_ref, vt_ref, gain_ref, o_ref.at[rows, :],
                      *scratch)
        return 0

    lax.fori_loop(0, q_ref.shape[0] // SB_QTILE, query_tile, 0)


def _sb_attn_tile(tile, q_ref, k_ref, vt_ref, gain_ref, o_ref, later_ref, acc_ref):
    n_heads = q_ref.shape[1] // HEAD_DIM
    first_head = lax.broadcasted_iota(jnp.int32, (LANES, SB_QTILE), 0) < HEAD_DIM
    q_pairs = []
    for p in range(n_heads // 2):
        q_t = q_ref[:, p * LANES:(p + 1) * LANES].astype(F32).T
        q_pairs.append(jnp.concatenate([jnp.where(first_head, q_t, 0.0),
                                        jnp.where(first_head, 0.0, q_t)], axis=1).astype(BF16))
    kk = lax.broadcasted_iota(jnp.int32, (SB_TILE, SB_TILE), 0)
    jj = lax.broadcasted_iota(jnp.int32, (SB_TILE, SB_TILE), 1)
    later_keys = jnp.where(jj > kk, 1.0, 0.0).astype(BF16)

    def key_tile(j, q_start, n_queries, masked):
        k_t = k_ref[pl.ds(pl.multiple_of(j * SB_TILE, SB_TILE), SB_TILE), :]
        v_t = vt_ref[j]
        q_cols = slice(q_start, q_start + n_queries)
        if masked:
            causal = (lax.broadcasted_iota(jnp.int32, (SB_TILE, n_queries), 0)
                      < lax.broadcasted_iota(jnp.int32, (SB_TILE, n_queries), 1))

        def logits(p):
            q_pair = jnp.concatenate([q_pairs[p][:, q_cols],
                                      q_pairs[p][:, SB_QTILE + q_start:SB_QTILE + q_start + n_queries]],
                                     axis=1)
            return jnp.dot(k_t[:, p * LANES:(p + 1) * LANES], q_pair, preferred_element_type=F32)

        def finish(h, log_beta, sums, key0_row):
            rows = slice(h * HEAD_DIM, (h + 1) * HEAD_DIM)
            cols = slice(h * SB_QTILE + q_start, h * SB_QTILE + q_start + n_queries)
            later = later_ref[:, cols]
            att = jnp.exp(log_beta + sums + later).astype(BF16)
            acc_ref[rows, q_cols] += jnp.dot(v_t[rows, :], att, preferred_element_type=F32)
            later_ref[:, cols] = later + sums[0:1, :] + key0_row

        z_pair = logits(0)
        pending = None
        for h in range(n_heads):
            if h % 2 == 0:
                z_now = z_pair
                if h + 2 < n_heads:
                    z_pair = logits(h // 2 + 1)
            z = z_now[:, (h % 2) * n_queries:(h % 2 + 1) * n_queries]
            neg_part = jnp.minimum(z, 0.0)
            neg_relu = neg_part - z
            soft = jnp.log(1.0 + jnp.exp(neg_part + neg_relu))
            log_beta = neg_part - soft
            log_1m_beta = neg_relu - soft
            if masked:
                log_beta = jnp.where(causal, log_beta, MASKED_LOG_BETA)
                log_1m_beta = jnp.where(causal, log_1m_beta, 0.0)
            sums = jnp.dot(later_keys, log_1m_beta.astype(BF16), preferred_element_type=F32)
            if pending is not None:
                finish(*pending)
            pending = (h, log_beta, sums, log_1m_beta[0:1, :])
        finish(*pending)

    later_ref[...] = jnp.zeros_like(later_ref)
    acc_ref[...] = jnp.zeros_like(acc_ref)
    ratio = SB_QTILE // SB_TILE
    for d in reversed(range(ratio)):
        key_tile(ratio * tile + d, d * SB_TILE, SB_QTILE - d * SB_TILE, masked=True)

    def earlier_tile(i, _):
        key_tile(ratio * tile - 1 - i, 0, SB_QTILE, masked=False)
        return 0

    lax.fori_loop(0, ratio * tile, earlier_tile, 0)

    def head_norm(acc):
        ms = jnp.sum(acc * acc, axis=0, keepdims=True) * (1.0 / HEAD_DIM)
        return acc * lax.rsqrt(ms + RMS_EPS)

    out_t = jnp.concatenate([head_norm(acc_ref[h * HEAD_DIM:(h + 1) * HEAD_DIM, :])
                             for h in range(n_heads)], axis=0)
    o_ref[...] = (out_t.T * gain_ref[...]).astype(o_ref.dtype)


def _sb_attn_call(q, k, v_t, out_gain):
    b, s, width = q.shape
    step_lanes = SB_HEADS_PER_STEP * HEAD_DIM
    assert s % SB_QTILE == 0 and width % step_lanes == 0
    seq_spec = pl.BlockSpec((None, s, step_lanes), lambda bi, p: (bi, 0, p))
    all_heads = SB_HEADS_PER_STEP * SB_QTILE
    return pl.pallas_call(
        _sb_attn_kernel,
        out_shape=jax.ShapeDtypeStruct((b, s, width), BF16),
        grid=(b, width // step_lanes),
        in_specs=[seq_spec, seq_spec,
                  pl.BlockSpec((s // SB_TILE, step_lanes, SB_TILE), lambda bi, p: (bi, p, 0)),
                  pl.BlockSpec((1, step_lanes), lambda bi, p: (0, p))],
        out_specs=seq_spec,
        scratch_shapes=[pltpu.VMEM((1, all_heads), F32),
                        pltpu.VMEM((step_lanes, SB_QTILE), F32)],
        compiler_params=pltpu.CompilerParams(
            dimension_semantics=("parallel", "parallel"), vmem_limit_bytes=VMEM_LIMIT),
        name="sb_attn",
    )(q, k, v_t, out_gain)


def _rwkv_kernel(u_ref, mu_ref, w0_ref, w2_ref, a0_ref, a2_ref, g2_ref, kk_ref, ka_ref,
                 rk_ref, lnw_ref, lnb_ref, ones_bd_ref, tri_ref, o_ref,
                 prev_ref, state_ref, r_s, kf_s, v_s, kn_s, eta_s, ld_s, y_s):
    n_rows = u_ref.shape[0]

    @pl.when(pl.program_id(1) == 0)
    def _():
        prev_ref[...] = jnp.zeros_like(prev_ref)
        state_ref[...] = jnp.zeros_like(state_ref)

    u = u_ref[...]
    row_id = lax.broadcasted_iota(jnp.int32, u.shape, 0)
    shifted = jnp.where(row_id == 0, prev_ref[...], pltpu.roll(u, 1, axis=0))
    prev_ref[...] = u[n_rows - 1:n_rows, :]
    u = u + (shifted - u) * mu_ref[...]

    x_r = u[:, 0:RW_WIDTH]
    x_k = u[:, RW_WIDTH:2 * RW_WIDTH]
    x_v = u[:, 2 * RW_WIDTH:3 * RW_WIDTH]
    x_wa = u[:, 3 * RW_WIDTH:3 * RW_WIDTH + DECAY_LORA + AAA_LORA]
    x_g = u[:, 3 * RW_WIDTH + DECAY_LORA + AAA_LORA:]
    ones_bd = ones_bd_ref[...]

    pre = -(w0_ref[...] + _dot_split(jnp.tanh(x_wa), w2_ref[0], w2_ref[1]))
    log_w = -(jnp.maximum(pre, 0.0) + jnp.log(1.0 + jnp.exp(-jnp.abs(pre)))) - 0.5
    ld_s[...] = -jnp.exp(log_w)
    eta = jax.nn.sigmoid(a0_ref[...] + _dot(x_wa, a2_ref[...]))
    gate = _dot(jax.nn.sigmoid(x_g), g2_ref[...])
    kk = x_k * kk_ref[...]
    kk_scale = jnp.minimum(lax.rsqrt(_head_reduce(kk * kk, ones_bd, two_pass=True)),
                           1.0 / KK_NORM_FLOOR)
    kf = x_k * (1.0 + (eta - 1.0) * ka_ref[...])
    bonus = _head_reduce(x_r * kf * rk_ref[...], ones_bd, two_pass=False) * x_v
    r_s[...] = x_r
    kf_s[...] = kf
    v_s[...] = x_v
    kn_s[...] = kk * kk_scale
    eta_s[...] = eta

    lane = lax.broadcasted_iota(jnp.int32, (RW_CHUNK, LANES), 1)
    first_head = lane < HEAD_DIM
    ri = lax.broadcasted_iota(jnp.int32, (LANES, LANES), 0)
    ci = lax.broadcasted_iota(jnp.int32, (LANES, LANES), 1)
    same_head = (ri < HEAD_DIM) == (ci < HEAD_DIM)
    strict_lower = same_head & (ci < ri)
    lower = same_head & (ci <= ri)
    eye = (ri == ci).astype(F32)
    tri = tri_ref[...]

    def stack_heads(t):
        return jnp.concatenate([jnp.where(first_head, t, 0.0), jnp.where(first_head, 0.0, t)],
                               axis=0)

    n_pairs = RW_WIDTH // LANES

    def chunk_operands(rows):
        ld = ld_s[rows, :]
        cum = _prefix_sum(tri, ld)
        total = cum[RW_CHUNK - 1:RW_CHUNK, :]
        e_incl = jnp.exp(cum)
        e_inv = jnp.exp(-cum)
        e_tail = jnp.exp(total - cum)
        w_total = jnp.exp(total)
        kn = kn_s[rows, :]
        kf_c = kf_s[rows, :]
        b = kn * eta_s[rows, :]
        a_dec = -kn * jnp.exp(cum - ld)
        r_dec = r_s[rows, :] * e_incl
        b_inv = b * e_inv
        k_inv = kf_c * e_inv
        b_tail = b * e_tail
        k_tail = kf_c * e_tail
        v_c = v_s[rows, :]
        units = []
        for p in range(n_pairs):
            cols = slice(p * LANES, (p + 1) * LANES)
            units.append(dict(
                a_sm=stack_heads(a_dec[:, cols]), r_sm=stack_heads(r_dec[:, cols]),
                v_sm=stack_heads(v_c[:, cols]), b_inv=b_inv[:, cols], k_inv=k_inv[:, cols],
                b_tail_t=stack_heads(b_tail[:, cols]).T, k_tail_t=stack_heads(k_tail[:, cols]).T,
                decay_diag=eye * w_total[:, cols]))
        return units

    def chunk_group(g, _):
        row_slices = [pl.ds(pl.multiple_of((g * RW_GROUP + c) * RW_CHUNK, RW_CHUNK), RW_CHUNK)
                      for c in range(RW_GROUP)]
        units = [u for rows in row_slices for u in chunk_operands(rows)]
        scores = [_dot_nt(jnp.concatenate([u["a_sm"], u["r_sm"]], axis=0),
                          jnp.concatenate([u["b_inv"], u["k_inv"]], axis=0))
                  for u in units]

        def block_diag(s, mask):
            swapped = pltpu.roll(s, RW_CHUNK, axis=1)
            x_bd = jnp.concatenate([s[:RW_CHUNK], swapped[RW_CHUNK:]], axis=0)
            y_bd = jnp.concatenate([swapped[:RW_CHUNK], s[RW_CHUNK:]], axis=0)
            return jnp.where(mask, x_bd, 0.0), jnp.where(mask, y_bd, 0.0)

        l_ab, l_ak = zip(*[block_diag(s[:LANES], strict_lower) for s in scores])
        m_rb, m_rk = zip(*[block_diag(s[LANES:], lower) for s in scores])
        lakv = [_dot(l, u["v_sm"]) for l, u in zip(l_ak, units)]
        inv = [eye + l for l in l_ab]
        power = [_dot(l, l) for l in l_ab]
        for _ in range(4):
            both = [_dot(pw, jnp.concatenate([pw, iv], axis=1)) for pw, iv in zip(power, inv)]
            inv = [iv + x[:, LANES:] for iv, x in zip(inv, both)]
            power = [x[:, :LANES] for x in both]
        inv = [iv + _dot(pw, iv) for iv, pw in zip(inv, power)]
        pu = [_dot(iv, jnp.concatenate([u["a_sm"], lv], axis=1))
              for iv, u, lv in zip(inv, units, lakv)]
        zeros = jnp.zeros((LANES, LANES), F32)
        tail = [_dot(jnp.concatenate([jnp.concatenate([mb, mk], axis=1),
                                      jnp.concatenate([u["b_tail_t"], u["k_tail_t"]], axis=1)],
                                     axis=0),
                     jnp.concatenate([x, jnp.concatenate([zeros, u["v_sm"]], axis=1)], axis=0))
                for mb, mk, u, x in zip(m_rb, m_rk, units, pu)]
        states = [state_ref[p] for p in range(n_pairs)]
        for c, rows in enumerate(row_slices):
            new_states = []
            for p in range(n_pairs):
                i = c * n_pairs + p
                q_sm = units[i]["r_sm"] + tail[i][:LANES, :LANES]
                y_sm = _dot(q_sm, states[p]) + tail[i][:LANES, LANES:]
                g_bd = tail[i][LANES:, :LANES] + units[i]["decay_diag"]
                new_states.append(_dot(g_bd, states[p]) + tail[i][LANES:, LANES:])
                y_s[rows, p * LANES:(p + 1) * LANES] = y_sm[:RW_CHUNK, :] + y_sm[RW_CHUNK:, :]
            states = new_states
        for p in range(n_pairs):
            state_ref[p] = states[p]
        return 0

    lax.fori_loop(0, n_rows // (RW_CHUNK * RW_GROUP), chunk_group, 0)

    y = y_s[...]
    centered = y - _head_reduce(y, ones_bd, two_pass=False) * (1.0 / HEAD_DIM)
    var = _head_reduce(centered * centered, ones_bd, two_pass=False) * (1.0 / HEAD_DIM)
    yn = centered * lax.rsqrt(var + LNX_EPS) * lnw_ref[...] + lnb_ref[...]
    o_ref[...] = ((yn + bonus) * gate).astype(o_ref.dtype)


def _rwkv_call(rw_in, mu, w0, w2_pad, a0, a2_pad, g2, k_k, k_a, r_k, ln_w, ln_b, ones_bd, tri):
    b, s, width = rw_in.shape
    row_tile = next(t for t in (512, 256, 128) if s % t == 0)
    vec = lambda n: _const_spec((1, n))
    scratch_rows = pltpu.VMEM((row_tile, RW_WIDTH), F32)
    return pl.pallas_call(
        _rwkv_kernel,
        out_shape=jax.ShapeDtypeStruct((b, s, RW_WIDTH), BF16),
        grid=(b, s // row_tile),
        in_specs=[pl.BlockSpec((None, row_tile, width), lambda bi, j: (bi, j, 0)),
                  vec(width), vec(RW_WIDTH), _const_spec(w2_pad.shape), vec(RW_WIDTH),
                  _const_spec(a2_pad.shape), _const_spec(g2.shape), vec(RW_WIDTH),
                  vec(RW_WIDTH), vec(RW_WIDTH), vec(RW_WIDTH), vec(RW_WIDTH),
                  _const_spec(ones_bd.shape), _const_spec(tri.shape)],
        out_specs=pl.BlockSpec((None, row_tile, RW_WIDTH), lambda bi, j: (bi, j, 0)),
        scratch_shapes=[pltpu.VMEM((1, width), F32),
                        pltpu.VMEM((RW_WIDTH // LANES, LANES, LANES), F32)]
                       + [scratch_rows] * 7,
        compiler_params=pltpu.CompilerParams(
            dimension_semantics=("parallel", "arbitrary"), vmem_limit_bytes=VMEM_LIMIT),
        name="rwkv7",
    )(rw_in, mu, w0, w2_pad, a0, a2_pad, g2, k_k, k_a, r_k, ln_w, ln_b, ones_bd, tri)


def _head_block_diag(value, dtype):
    idx = jnp.arange(MXU_DIM) // HEAD_DIM
    return jnp.where(idx[:, None] == idx[None, :], value, 0.0).astype(dtype)


def kernel(x, norm_ffn1, ffn1_gate, ffn1_up, ffn1_down, norm_mix, w_in, sb_q_norm, sb_k_norm,
           sb_out_norm, rw_mu, rw_w0, rw_w2, rw_a0, rw_a2, rw_g2, rw_k_k, rw_k_a, rw_r_k,
           rw_ln_w, rw_ln_b, w_out, norm_ffn2, ffn2_gate, ffn2_up, ffn2_down):
    b, s, d = x.shape
    n = b * s
    heads = SB_WIDTH // HEAD_DIM
    mean_bd = _head_block_diag(1.0 / HEAD_DIM, BF16)
    ones_bd = _head_block_diag(1.0, BF16)
    tri = jnp.tril(jnp.ones((RW_CHUNK, RW_CHUNK), BF16))
    row = lambda t: t.reshape(1, -1)

    h = x.reshape(n, d)
    for l in range(norm_ffn1.shape[0]):
        h = _ffn_call(h, row(norm_ffn1[l]), ffn1_gate[l].astype(BF16), ffn1_up[l].astype(BF16),
                      ffn1_down[l].astype(BF16))

        w_in_l = w_in[l].astype(BF16)
        q, k, v_t, rw_in = _mix_in_call(
            h, row(norm_mix[l]), w_in_l, w_in_l[:, 2 * SB_WIDTH:3 * SB_WIDTH].T,
            row(jnp.tile(sb_q_norm[l], heads)), row(jnp.tile(sb_k_norm[l], heads)), mean_bd)

        o_sb = _sb_attn_call(q.reshape(b, s, SB_WIDTH), k.reshape(b, s, SB_WIDTH), v_t,
                             row(sb_out_norm[l]))

        zeros = jnp.zeros((DECAY_LORA, RW_WIDTH), F32)
        w2_pad = jnp.stack(_split_bf16(jnp.concatenate([rw_w2[l], zeros], axis=0)))
        a2_pad = jnp.concatenate([zeros, rw_a2[l]], axis=0).astype(BF16)
        o_rw = _rwkv_call(rw_in.reshape(b, s, RW_PROJ), row(rw_mu[l]), row(rw_w0[l]), w2_pad,
                          row(rw_a0[l]), a2_pad, rw_g2[l].astype(BF16), row(rw_k_k[l]),
                          row(rw_k_a[l]), row(rw_r_k[l]), row(rw_ln_w[l]), row(rw_ln_b[l]),
                          ones_bd, tri)

        h = _ffn_call(h, row(norm_ffn2[l]), ffn2_gate[l].astype(BF16), ffn2_up[l].astype(BF16),
                      ffn2_down[l].astype(BF16),
                      proj=(o_sb.reshape(n, SB_WIDTH), o_rw.reshape(n, RW_WIDTH),
                            w_out[l].astype(BF16)))
    return h.reshape(b, s, d)
```

```python
import functools

import jax
import jax.numpy as jnp
from jax import lax
from jax.experimental import pallas as pl
from jax.experimental.pallas import tpu as pltpu

F32 = jnp.float32
BF16 = jnp.bfloat16

HEAD_DIM = 64
SB_WIDTH = 512
RW_WIDTH = 512
DECAY_LORA = 64
AAA_LORA = 64
GATE_LORA = 128
RW_PROJ = 3 * RW_WIDTH + DECAY_LORA + AAA_LORA + GATE_LORA
RMS_EPS = 1e-6
LNX_EPS = 64e-5
KK_NORM_FLOOR = 1e-12

LANES = 128
MXU_DIM = 256
RW_CHUNK = 64
RW_GROUP = 4
SB_TILE = 256
SB_QTILE = 2 * SB_TILE
SB_HEADS_PER_STEP = 8
MASKED_LOG_BETA = -1e30
VMEM_LIMIT = 56 * 1024 * 1024


def _const_spec(shape):
    zeros = (0,) * len(shape)
    return pl.BlockSpec(shape, lambda *_: zeros, pipeline_mode=pl.Buffered(1))


def _dot(a, b):
    return jnp.dot(a.astype(BF16), b.astype(BF16), preferred_element_type=F32)


def _dot_nt(a, b):
    return lax.dot_general(a.astype(BF16), b.astype(BF16), (((1,), (1,)), ((), ())),
                           preferred_element_type=F32)


def _split_bf16(x):
    hi = x.astype(BF16)
    return hi, (x - hi.astype(F32)).astype(BF16)


def _dot_split(a, b_hi, b_lo):
    a_hi, a_lo = _split_bf16(a)
    return (jnp.dot(a_hi, b_hi, preferred_element_type=F32)
            + jnp.dot(a_hi, b_lo, preferred_element_type=F32)
            + jnp.dot(a_lo, b_hi, preferred_element_type=F32))


def _prefix_sum(tri, x):
    hi = x.astype(BF16)
    rest = x - hi.astype(F32)
    mid = rest.astype(BF16)
    lo = (rest - mid.astype(F32)).astype(BF16)
    return (jnp.dot(tri, hi, preferred_element_type=F32)
            + jnp.dot(tri, mid, preferred_element_type=F32)
            + jnp.dot(tri, lo, preferred_element_type=F32))


def _rms_norm_rows(x, gain):
    ms = jnp.mean(x * x, axis=-1, keepdims=True)
    return x * lax.rsqrt(ms + RMS_EPS) * gain


def _head_reduce(t, bd, two_pass):
    outs = []
    for half in range(t.shape[1] // MXU_DIM):
        th = t[:, half * MXU_DIM:(half + 1) * MXU_DIM]
        hi = th.astype(BF16)
        red = jnp.dot(hi, bd, preferred_element_type=F32)
        if two_pass:
            lo = (th - hi.astype(F32)).astype(BF16)
            red = red + jnp.dot(lo, bd, preferred_element_type=F32)
        outs.append(red)
    return jnp.concatenate(outs, axis=1)


def _swiglu_residual(x, gain_ref, wg_ref, wu_ref, wd_ref, ff_tile):
    h = _rms_norm_rows(x, gain_ref[...]).astype(BF16)
    acc = x
    for start in range(0, wg_ref.shape[1], ff_tile):
        cols = slice(start, min(start + ff_tile, wg_ref.shape[1]))
        gate = jnp.dot(h, wg_ref[:, cols], preferred_element_type=F32)
        up = jnp.dot(h, wu_ref[:, cols], preferred_element_type=F32)
        act = (gate * jax.nn.sigmoid(gate) * up * 0.5).astype(BF16)
        acc = acc + jnp.dot(act, wd_ref[cols, :], preferred_element_type=F32)
    return acc


def _ffn_kernel(x_ref, gain_ref, wg_ref, wu_ref, wd_ref, o_ref, *, ff_tile):
    o_ref[...] = _swiglu_residual(x_ref[...], gain_ref, wg_ref, wu_ref, wd_ref, ff_tile)


def _proj_ffn_kernel(x_ref, osb_ref, orw_ref, wo_ref, gain_ref, wg_ref, wu_ref, wd_ref,
                     o_ref, *, ff_tile):
    mixed = (jnp.dot(osb_ref[...], wo_ref[:SB_WIDTH, :], preferred_element_type=F32)
             + jnp.dot(orw_ref[...], wo_ref[SB_WIDTH:, :], preferred_element_type=F32))
    x = mixed + x_ref[...]
    o_ref[...] = _swiglu_residual(x, gain_ref, wg_ref, wu_ref, wd_ref, ff_tile)


def _ffn_tiles(n_tokens, d_ff):
    row_tile = 512 if n_tokens % 512 == 0 else n_tokens
    ff_tile = -(-d_ff // (2 * MXU_DIM)) * MXU_DIM if d_ff % MXU_DIM == 0 else d_ff
    return row_tile, ff_tile


def _ffn_call(x, gain, wg, wu, wd, proj=None):
    n, d = x.shape
    d_ff = wg.shape[1]
    row_tile, ff_tile = _ffn_tiles(n, d_ff)
    row_spec = pl.BlockSpec((row_tile, d), lambda i: (i, 0))
    weight_specs = [_const_spec((1, d)), _const_spec((d, d_ff)), _const_spec((d, d_ff)),
                    _const_spec((d_ff, d))]
    if proj is None:
        body = functools.partial(_ffn_kernel, ff_tile=ff_tile)
        in_specs = [row_spec] + weight_specs
        args = (x, gain, wg, wu, wd)
        name = "ffn1"
    else:
        o_sb, o_rw, w_out = proj
        body = functools.partial(_proj_ffn_kernel, ff_tile=ff_tile)
        in_specs = [row_spec,
                    pl.BlockSpec((row_tile, SB_WIDTH), lambda i: (i, 0)),
                    pl.BlockSpec((row_tile, RW_WIDTH), lambda i: (i, 0)),
                    _const_spec(w_out.shape)] + weight_specs
        args = (x, o_sb, o_rw, w_out, gain, wg, wu, wd)
        name = "ffn2"
    return pl.pallas_call(
        body,
        out_shape=jax.ShapeDtypeStruct((n, d), F32),
        grid=(n // row_tile,),
        in_specs=in_specs,
        out_specs=row_spec,
        compiler_params=pltpu.CompilerParams(
            dimension_semantics=("parallel",), vmem_limit_bytes=VMEM_LIMIT),
        name=name,
    )(*args)


def _mix_in_kernel(x_ref, gain_ref, w_ref, qg_ref, kg_ref, mean_bd_ref,
                   q_ref, k_ref, vt_ref, rw_ref):
    h = _rms_norm_rows(x_ref[...], gain_ref[...]).astype(BF16)
    mean_bd = mean_bd_ref[...]

    def head_norm(t, gain):
        ms = _head_reduce(t * t, mean_bd, two_pass=False)
        return t * lax.rsqrt(ms + RMS_EPS) * gain

    q = jnp.dot(h, w_ref[:, 0:SB_WIDTH], preferred_element_type=F32)
    q_ref[...] = (head_norm(q, qg_ref[...]) * (HEAD_DIM ** -0.5)).astype(BF16)
    k = jnp.dot(h, w_ref[:, SB_WIDTH:2 * SB_WIDTH], preferred_element_type=F32)
    k_ref[...] = head_norm(k, kg_ref[...]).astype(BF16)
    vt = lax.dot_general(w_ref[:, 2 * SB_WIDTH:3 * SB_WIDTH], h, (((0,), (1,)), ((), ())),
                         preferred_element_type=F32).astype(BF16)
    for t in range(vt_ref.shape[0]):
        vt_ref[t] = vt[:, t * SB_TILE:(t + 1) * SB_TILE]
    rw_ref[...] = jnp.dot(h, w_ref[:, 3 * SB_WIDTH:], preferred_element_type=F32)


def _mix_in_call(x, gain, w_in, q_gain, k_gain, mean_bd):
    n, d = x.shape
    row_tile = 2 * SB_TILE
    assert n % row_tile == 0
    row = lambda width: pl.BlockSpec((row_tile, width), lambda i: (i, 0))
    return pl.pallas_call(
        _mix_in_kernel,
        out_shape=(jax.ShapeDtypeStruct((n, SB_WIDTH), BF16),
                   jax.ShapeDtypeStruct((n, SB_WIDTH), BF16),
                   jax.ShapeDtypeStruct((n // SB_TILE, SB_WIDTH, SB_TILE), BF16),
                   jax.ShapeDtypeStruct((n, RW_PROJ), F32)),
        grid=(n // row_tile,),
        in_specs=[row(d), _const_spec((1, d)), _const_spec(w_in.shape),
                  _const_spec((1, SB_WIDTH)), _const_spec((1, SB_WIDTH)),
                  _const_spec(mean_bd.shape)],
        out_specs=(row(SB_WIDTH), row(SB_WIDTH),
                   pl.BlockSpec((row_tile // SB_TILE, SB_WIDTH, SB_TILE), lambda i: (i, 0, 0)),
                   row(RW_PROJ)),
        compiler_params=pltpu.CompilerParams(
            dimension_semantics=("parallel",), vmem_limit_bytes=VMEM_LIMIT),
        name="mix_in",
    )(x, gain, w_in, q_gain, k_gain, mean_bd)


def _sb_attn_kernel(q_ref, k_ref, vt_ref, gain_ref, o_ref, *scratch):
    def query_tile(tile, _):
        rows = pl.ds(pl.multiple_of(tile * SB_QTILE, SB_QTILE), SB_QTILE)
        _sb_attn_tile(tile, q_ref.at[rows, :], k_ref, vt_ref, gain_ref, o_ref.at[rows, :],
                      *scratch)
        return 0

    lax.fori_loop(0, q_ref.shape[0] // SB_QTILE, query_tile, 0)


def _sb_attn_tile(tile, q_ref, k_ref, vt_ref, gain_ref, o_ref, later_ref, acc_ref):
    n_heads = q_ref.shape[1] // HEAD_DIM
    first_head = lax.broadcasted_iota(jnp.int32, (LANES, SB_QTILE), 0) < HEAD_DIM
    q_pairs = []
    for p in range(n_heads // 2):
        q_t = q_ref[:, p * LANES:(p + 1) * LANES].astype(F32).T
        q_pairs.append(jnp.concatenate([jnp.where(first_head, q_t, 0.0),
                                        jnp.where(first_head, 0.0, q_t)], axis=1).astype(BF16))
    kk = lax.broadcasted_iota(jnp.int32, (SB_TILE, SB_TILE), 0)
    jj = lax.broadcasted_iota(jnp.int32, (SB_TILE, SB_TILE), 1)
    later_keys = jnp.where(jj > kk, 1.0, 0.0).astype(BF16)

    def key_tile(j, q_start, n_queries, masked):
        k_t = k_ref[pl.ds(pl.multiple_of(j * SB_TILE, SB_TILE), SB_TILE), :]
        v_t = vt_ref[j]
        q_cols = slice(q_start, q_start + n_queries)
        if masked:
            causal = (lax.broadcasted_iota(jnp.int32, (SB_TILE, n_queries), 0)
                      < lax.broadcasted_iota(jnp.int32, (SB_TILE, n_queries), 1))

        def logits(p):
            q_pair = jnp.concatenate([q_pairs[p][:, q_cols],
                                      q_pairs[p][:, SB_QTILE + q_start:SB_QTILE + q_start + n_queries]],
                                     axis=1)
            return jnp.dot(k_t[:, p * LANES:(p + 1) * LANES], q_pair, preferred_element_type=F32)

        def finish(h, log_beta, sums, key0_row):
            rows = slice(h * HEAD_DIM, (h + 1) * HEAD_DIM)
            cols = slice(h * SB_QTILE + q_start, h * SB_QTILE + q_start + n_queries)
            later = later_ref[:, cols]
            att = jnp.exp(log_beta + sums + later).astype(BF16)
            acc_ref[rows, q_cols] += jnp.dot(v_t[rows, :], att, preferred_element_type=F32)
            later_ref[:, cols] = later + sums[0:1, :] + key0_row

        z_pair = logits(0)
        pending = None
        for h in range(n_heads):
            if h % 2 == 0:
                z_now = z_pair
                if h + 2 < n_heads:
                    z_pair = logits(h // 2 + 1)
            z = z_now[:, (h % 2) * n_queries:(h % 2 + 1) * n_queries]
            neg_part = jnp.minimum(z, 0.0)
            neg_relu = neg_part - z
            soft = jnp.log(1.0 + jnp.exp(neg_part + neg_relu))
            log_beta = neg_part - soft
            log_1m_beta = neg_relu - soft
            if masked:
                log_beta = jnp.where(causal, log_beta, MASKED_LOG_BETA)
                log_1m_beta = jnp.where(causal, log_1m_beta, 0.0)
            sums = jnp.dot(later_keys, log_1m_beta.astype(BF16), preferred_element_type=F32)
            if pending is not None:
                finish(*pending)
            pending = (h, log_beta, sums, log_1m_beta[0:1, :])
        finish(*pending)

    later_ref[...] = jnp.zeros_like(later_ref)
    acc_ref[...] = jnp.zeros_like(acc_ref)
    ratio = SB_QTILE // SB_TILE
    for d in reversed(range(ratio)):
        key_tile(ratio * tile + d, d * SB_TILE, SB_QTILE - d * SB_TILE, masked=True)

    def earlier_tile(i, _):
        key_tile(ratio * tile - 1 - i, 0, SB_QTILE, masked=False)
        return 0

    lax.fori_loop(0, ratio * tile, earlier_tile, 0)

    def head_norm(acc):
        ms = jnp.sum(acc * acc, axis=0, keepdims=True) * (1.0 / HEAD_DIM)
        return acc * lax.rsqrt(ms + RMS_EPS)

    out_t = jnp.concatenate([head_norm(acc_ref[h * HEAD_DIM:(h + 1) * HEAD_DIM, :])
                             for h in range(n_heads)], axis=0)
    o_ref[...] = (out_t.T * gain_ref[...]).astype(o_ref.dtype)


def _sb_attn_call(q, k, v_t, out_gain):
    b, s, width = q.shape
    step_lanes = SB_HEADS_PER_STEP * HEAD_DIM
    assert s % SB_QTILE == 0 and width % step_lanes == 0
    seq_spec = pl.BlockSpec((None, s, step_lanes), lambda bi, p: (bi, 0, p))
    all_heads = SB_HEADS_PER_STEP * SB_QTILE
    return pl.pallas_call(
        _sb_attn_kernel,
        out_shape=jax.ShapeDtypeStruct((b, s, width), BF16),
        grid=(b, width // step_lanes),
        in_specs=[seq_spec, seq_spec,
                  pl.BlockSpec((s // SB_TILE, step_lanes, SB_TILE), lambda bi, p: (bi, p, 0)),
                  pl.BlockSpec((1, step_lanes), lambda bi, p: (0, p))],
        out_specs=seq_spec,
        scratch_shapes=[pltpu.VMEM((1, all_heads), F32),
                        pltpu.VMEM((step_lanes, SB_QTILE), F32)],
        compiler_params=pltpu.CompilerParams(
            dimension_semantics=("parallel", "parallel"), vmem_limit_bytes=VMEM_LIMIT),
        name="sb_attn",
    )(q, k, v_t, out_gain)


def _rwkv_kernel(u_ref, mu_ref, w0_ref, w2_ref, a0_ref, a2_ref, g2_ref, kk_ref, ka_ref,
                 rk_ref, lnw_ref, lnb_ref, ones_bd_ref, tri_ref, o_ref,
                 prev_ref, state_ref, r_s, kf_s, v_s, kn_s, eta_s, ld_s, y_s):
    n_rows = u_ref.shape[0]

    @pl.when(pl.program_id(1) == 0)
    def _():
        prev_ref[...] = jnp.zeros_like(prev_ref)
        state_ref[...] = jnp.zeros_like(state_ref)

    u = u_ref[...]
    row_id = lax.broadcasted_iota(jnp.int32, u.shape, 0)
    shifted = jnp.where(row_id == 0, prev_ref[...], pltpu.roll(u, 1, axis=0))
    prev_ref[...] = u[n_rows - 1:n_rows, :]
    u = u + (shifted - u) * mu_ref[...]

    x_r = u[:, 0:RW_WIDTH]
    x_k = u[:, RW_WIDTH:2 * RW_WIDTH]
    x_v = u[:, 2 * RW_WIDTH:3 * RW_WIDTH]
    x_wa = u[:, 3 * RW_WIDTH:3 * RW_WIDTH + DECAY_LORA + AAA_LORA]
    x_g = u[:, 3 * RW_WIDTH + DECAY_LORA + AAA_LORA:]
    ones_bd = ones_bd_ref[...]

    pre = -(w0_ref[...] + _dot_split(jnp.tanh(x_wa), w2_ref[0], w2_ref[1]))
    log_w = -(jnp.maximum(pre, 0.0) + jnp.log(1.0 + jnp.exp(-jnp.abs(pre)))) - 0.5
    ld_s[...] = -jnp.exp(log_w)
    eta = jax.nn.sigmoid(a0_ref[...] + _dot(x_wa, a2_ref[...]))
    gate = _dot(jax.nn.sigmoid(x_g), g2_ref[...])
    kk = x_k * kk_ref[...]
    kk_scale = jnp.minimum(lax.rsqrt(_head_reduce(kk * kk, ones_bd, two_pass=True)),
                           1.0 / KK_NORM_FLOOR)
    kf = x_k * (1.0 + (eta - 1.0) * ka_ref[...])
    bonus = _head_reduce(x_r * kf * rk_ref[...], ones_bd, two_pass=False) * x_v
    r_s[...] = x_r
    kf_s[...] = kf
    v_s[...] = x_v
    kn_s[...] = kk * kk_scale
    eta_s[...] = eta

    lane = lax.broadcasted_iota(jnp.int32, (RW_CHUNK, LANES), 1)
    first_head = lane < HEAD_DIM
    ri = lax.broadcasted_iota(jnp.int32, (LANES, LANES), 0)
    ci = lax.broadcasted_iota(jnp.int32, (LANES, LANES), 1)
    same_head = (ri < HEAD_DIM) == (ci < HEAD_DIM)
    strict_lower = same_head & (ci < ri)
    lower = same_head & (ci <= ri)
    eye = (ri == ci).astype(F32)
    tri = tri_ref[...]

    def stack_heads(t):
        return jnp.concatenate([jnp.where(first_head, t, 0.0), jnp.where(first_head, 0.0, t)],
                               axis=0)

    n_pairs = RW_WIDTH // LANES

    def chunk_operands(rows):
        ld = ld_s[rows, :]
        cum = _prefix_sum(tri, ld)
        total = cum[RW_CHUNK - 1:RW_CHUNK, :]
        e_incl = jnp.exp(cum)
        e_inv = jnp.exp(-cum)
        e_tail = jnp.exp(total - cum)
        w_total = jnp.exp(total)
        kn = kn_s[rows, :]
        kf_c = kf_s[rows, :]
        b = kn * eta_s[rows, :]
        a_dec = -kn * jnp.exp(cum - ld)
        r_dec = r_s[rows, :] * e_incl
        b_inv = b * e_inv
        k_inv = kf_c * e_inv
        b_tail = b * e_tail
        k_tail = kf_c * e_tail
        v_c = v_s[rows, :]
        units = []
        for p in range(n_pairs):
            cols = slice(p * LANES, (p + 1) * LANES)
            units.append(dict(
                a_sm=stack_heads(a_dec[:, cols]), r_sm=stack_heads(r_dec[:, cols]),
                v_sm=stack_heads(v_c[:, cols]), b_inv=b_inv[:, cols], k_inv=k_inv[:, cols],
                b_tail_t=stack_heads(b_tail[:, cols]).T, k_tail_t=stack_heads(k_tail[:, cols]).T,
                decay_diag=eye * w_total[:, cols]))
        return units

    def chunk_group(g, _):
        row_slices = [pl.ds(pl.multiple_of((g * RW_GROUP + c) * RW_CHUNK, RW_CHUNK), RW_CHUNK)
                      for c in range(RW_GROUP)]
        units = [u for rows in row_slices for u in chunk_operands(rows)]
        scores = [_dot_nt(jnp.concatenate([u["a_sm"], u["r_sm"]], axis=0),
                          jnp.concatenate([u["b_inv"], u["k_inv"]], axis=0))
                  for u in units]

        def block_diag(s, mask):
            swapped = pltpu.roll(s, RW_CHUNK, axis=1)
            x_bd = jnp.concatenate([s[:RW_CHUNK], swapped[RW_CHUNK:]], axis=0)
            y_bd = jnp.concatenate([swapped[:RW_CHUNK], s[RW_CHUNK:]], axis=0)
            return jnp.where(mask, x_bd, 0.0), jnp.where(mask, y_bd, 0.0)

        l_ab, l_ak = zip(*[block_diag(s[:LANES], strict_lower) for s in scores])
        m_rb, m_rk = zip(*[block_diag(s[LANES:], lower) for s in scores])
        lakv = [_dot(l, u["v_sm"]) for l, u in zip(l_ak, units)]
        inv = [eye + l for l in l_ab]
        power = [_dot(l, l) for l in l_ab]
        for _ in range(4):
            both = [_dot(pw, jnp.concatenate([pw, iv], axis=1)) for pw, iv in zip(power, inv)]
            inv = [iv + x[:, LANES:] for iv, x in zip(inv, both)]
            power = [x[:, :LANES] for x in both]
        inv = [iv + _dot(pw, iv) for iv, pw in zip(inv, power)]
        pu = [_dot(iv, jnp.concatenate([u["a_sm"], lv], axis=1))
              for iv, u, lv in zip(inv, units, lakv)]
        zeros = jnp.zeros((LANES, LANES), F32)
        tail = [_dot(jnp.concatenate([jnp.concatenate([mb, mk], axis=1),
                                      jnp.concatenate([u["b_tail_t"], u["k_tail_t"]], axis=1)],
                                     axis=0),
                     jnp.concatenate([x, jnp.concatenate([zeros, u["v_sm"]], axis=1)], axis=0))
                for mb, mk, u, x in zip(m_rb, m_rk, units, pu)]
        states = [state_ref[p] for p in range(n_pairs)]
        for c, rows in enumerate(row_slices):
            new_states = []
            for p in range(n_pairs):
                i = c * n_pairs + p
                q_sm = units[i]["r_sm"] + tail[i][:LANES, :LANES]
                y_sm = _dot(q_sm, states[p]) + tail[i][:LANES, LANES:]
                g_bd = tail[i][LANES:, :LANES] + units[i]["decay_diag"]
                new_states.append(_dot(g_bd, states[p]) + tail[i][LANES:, LANES:])
                y_s[rows, p * LANES:(p + 1) * LANES] = y_sm[:RW_CHUNK, :] + y_sm[RW_CHUNK:, :]
            states = new_states
        for p in range(n_pairs):
            state_ref[p] = states[p]
        return 0

    lax.fori_loop(0, n_rows // (RW_CHUNK * RW_GROUP), chunk_group, 0)

    y = y_s[...]
    centered = y - _head_reduce(y, ones_bd, two_pass=False) * (1.0 / HEAD_DIM)
    var = _head_reduce(centered * centered, ones_bd, two_pass=False) * (1.0 / HEAD_DIM)
    yn = centered * lax.rsqrt(var + LNX_EPS) * lnw_ref[...] + lnb_ref[...]
    o_ref[...] = ((yn + bonus) * gate).astype(o_ref.dtype)


def _rwkv_call(rw_in, mu, w0, w2_pad, a0, a2_pad, g2, k_k, k_a, r_k, ln_w, ln_b, ones_bd, tri):
    b, s, width = rw_in.shape
    row_tile = next(t for t in (512, 256, 128) if s % t == 0)
    vec = lambda n: _const_spec((1, n))
    scratch_rows = pltpu.VMEM((row_tile, RW_WIDTH), F32)
    return pl.pallas_call(
        _rwkv_kernel,
        out_shape=jax.ShapeDtypeStruct((b, s, RW_WIDTH), BF16),
        grid=(b, s // row_tile),
        in_specs=[pl.BlockSpec((None, row_tile, width), lambda bi, j: (bi, j, 0)),
                  vec(width), vec(RW_WIDTH), _const_spec(w2_pad.shape), vec(RW_WIDTH),
                  _const_spec(a2_pad.shape), _const_spec(g2.shape), vec(RW_WIDTH),
                  vec(RW_WIDTH), vec(RW_WIDTH), vec(RW_WIDTH), vec(RW_WIDTH),
                  _const_spec(ones_bd.shape), _const_spec(tri.shape)],
        out_specs=pl.BlockSpec((None, row_tile, RW_WIDTH), lambda bi, j: (bi, j, 0)),
        scratch_shapes=[pltpu.VMEM((1, width), F32),
                        pltpu.VMEM((RW_WIDTH // LANES, LANES, LANES), F32)]
                       + [scratch_rows] * 7,
        compiler_params=pltpu.CompilerParams(
            dimension_semantics=("parallel", "arbitrary"), vmem_limit_bytes=VMEM_LIMIT),
        name="rwkv7",
    )(rw_in, mu, w0, w2_pad, a0, a2_pad, g2, k_k, k_a, r_k, ln_w, ln_b, ones_bd, tri)


def _head_block_diag(value, dtype):
    idx = jnp.arange(MXU_DIM) // HEAD_DIM
    return jnp.where(idx[:, None] == idx[None, :], value, 0.0).astype(dtype)


def kernel(x, norm_ffn1, ffn1_gate, ffn1_up, ffn1_down, norm_mix, w_in, sb_q_norm, sb_k_norm,
           sb_out_norm, rw_mu, rw_w0, rw_w2, rw_a0, rw_a2, rw_g2, rw_k_k, rw_k_a, rw_r_k,
           rw_ln_w, rw_ln_b, w_out, norm_ffn2, ffn2_gate, ffn2_up, ffn2_down):
    b, s, d = x.shape
    n = b * s
    heads = SB_WIDTH // HEAD_DIM
    mean_bd = _head_block_diag(1.0 / HEAD_DIM, BF16)
    ones_bd = _head_block_diag(1.0, BF16)
    tri = jnp.tril(jnp.ones((RW_CHUNK, RW_CHUNK), BF16))
    row = lambda t: t.reshape(1, -1)

    h = x.reshape(n, d)
    for l in range(norm_ffn1.shape[0]):
        h = _ffn_call(h, row(norm_ffn1[l]), ffn1_gate[l].astype(BF16), ffn1_up[l].astype(BF16),
                      ffn1_down[l].astype(BF16))

        q, k, v_t, rw_in = _mix_in_call(
            h, row(norm_mix[l]), w_in[l].astype(BF16),
            row(jnp.tile(sb_q_norm[l], heads)), row(jnp.tile(sb_k_norm[l], heads)), mean_bd)

        o_sb = _sb_attn_call(q.reshape(b, s, SB_WIDTH), k.reshape(b, s, SB_WIDTH), v_t,
                             row(sb_out_norm[l]))

        zeros = jnp.zeros((DECAY_LORA, RW_WIDTH), F32)
        w2_pad = jnp.stack(_split_bf16(jnp.concatenate([rw_w2[l], zeros], axis=0)))
        a2_pad = jnp.concatenate([zeros, rw_a2[l]], axis=0).astype(BF16)
        o_rw = _rwkv_call(rw_in.reshape(b, s, RW_PROJ), row(rw_mu[l]), row(rw_w0[l]), w2_pad,
                          row(rw_a0[l]), a2_pad, rw_g2[l].astype(BF16), row(rw_k_k[l]),
                          row(rw_k_a[l]), row(rw_r_k[l]), row(rw_ln_w[l]), row(rw_ln_b[l]),
                          ones_bd, tri)

        h = _ffn_call(h, row(norm_ffn2[l]), ffn2_gate[l].astype(BF16), ffn2_up[l].astype(BF16),
                      ffn2_down[l].astype(BF16),
                      proj=(o_sb.reshape(n, SB_WIDTH), o_rw.reshape(n, RW_WIDTH),
                            w_out[l].astype(BF16)))
    return h.reshape(b, s, d)
```

```python
import functools

import jax
import jax.numpy as jnp
from jax import lax
from jax.experimental import pallas as pl
from jax.experimental.pallas import tpu as pltpu

F32 = jnp.float32
BF16 = jnp.bfloat16

HEAD_DIM = 64
SB_WIDTH = 512
RW_WIDTH = 512
DECAY_LORA = 64
AAA_LORA = 64
GATE_LORA = 128
RW_PROJ = 3 * RW_WIDTH + DECAY_LORA + AAA_LORA + GATE_LORA
RMS_EPS = 1e-6
LNX_EPS = 64e-5
KK_NORM_FLOOR = 1e-12

LANES = 128
MXU_DIM = 256
RW_CHUNK = 64
RW_GROUP = 4
SB_TILE = 256
SB_QTILE = 2 * SB_TILE
SB_HEADS_PER_STEP = 8
MASKED_LOG_BETA = -1e30
V7X_VMEM_BYTES = 64 * 1024 * 1024
VMEM_LIMIT = V7X_VMEM_BYTES * 3 // 4


def _const_spec(shape):
    zeros = (0,) * len(shape)
    return pl.BlockSpec(shape, lambda *_: zeros, pipeline_mode=pl.Buffered(1))


def _dot(a, b):
    return jnp.dot(a.astype(BF16), b.astype(BF16), preferred_element_type=F32)


def _dot_nt(a, b):
    return lax.dot_general(a.astype(BF16), b.astype(BF16), (((1,), (1,)), ((), ())),
                           preferred_element_type=F32)


def _split_bf16(x):
    hi = x.astype(BF16)
    return hi, (x - hi.astype(F32)).astype(BF16)


def _dot_split(a, b_hi, b_lo):
    a_hi, a_lo = _split_bf16(a)
    return (jnp.dot(a_hi, b_hi, preferred_element_type=F32)
            + jnp.dot(a_hi, b_lo, preferred_element_type=F32)
            + jnp.dot(a_lo, b_hi, preferred_element_type=F32))


def _prefix_sum(tri, x):
    hi = x.astype(BF16)
    rest = x - hi.astype(F32)
    mid = rest.astype(BF16)
    lo = (rest - mid.astype(F32)).astype(BF16)
    return (jnp.dot(tri, hi, preferred_element_type=F32)
            + jnp.dot(tri, mid, preferred_element_type=F32)
            + jnp.dot(tri, lo, preferred_element_type=F32))


def _rms_norm_rows(x, gain):
    ms = jnp.mean(x * x, axis=-1, keepdims=True)
    return x * lax.rsqrt(ms + RMS_EPS) * gain


def _head_reduce(t, bd, two_pass):
    outs = []
    for half in range(t.shape[1] // MXU_DIM):
        th = t[:, half * MXU_DIM:(half + 1) * MXU_DIM]
        hi = th.astype(BF16)
        red = jnp.dot(hi, bd, preferred_element_type=F32)
        if two_pass:
            lo = (th - hi.astype(F32)).astype(BF16)
            red = red + jnp.dot(lo, bd, preferred_element_type=F32)
        outs.append(red)
    return jnp.concatenate(outs, axis=1)


def _swiglu_residual(x, gain_ref, wg_ref, wu_ref, wd_ref, ff_tile):
    h = _rms_norm_rows(x, gain_ref[...]).astype(BF16)
    acc = x
    for start in range(0, wg_ref.shape[1], ff_tile):
        cols = slice(start, min(start + ff_tile, wg_ref.shape[1]))
        gate = jnp.dot(h, wg_ref[:, cols], preferred_element_type=F32)
        up = jnp.dot(h, wu_ref[:, cols], preferred_element_type=F32)
        act = (gate * jax.nn.sigmoid(gate) * up * 0.5).astype(BF16)
        acc = acc + jnp.dot(act, wd_ref[cols, :], preferred_element_type=F32)
    return acc


def _ffn_kernel(x_ref, gain_ref, wg_ref, wu_ref, wd_ref, o_ref, *, ff_tile):
    o_ref[...] = _swiglu_residual(x_ref[...], gain_ref, wg_ref, wu_ref, wd_ref, ff_tile)


def _proj_ffn_kernel(x_ref, osb_ref, orw_ref, wo_ref, gain_ref, wg_ref, wu_ref, wd_ref,
                     o_ref, *, ff_tile):
    mixed = (jnp.dot(osb_ref[...], wo_ref[:SB_WIDTH, :], preferred_element_type=F32)
             + jnp.dot(orw_ref[...], wo_ref[SB_WIDTH:, :], preferred_element_type=F32))
    x = mixed + x_ref[...]
    o_ref[...] = _swiglu_residual(x, gain_ref, wg_ref, wu_ref, wd_ref, ff_tile)


def _ffn_tiles(n_tokens, d_ff):
    row_tile = 512 if n_tokens % 512 == 0 else n_tokens
    ff_tile = -(-d_ff // (2 * MXU_DIM)) * MXU_DIM if d_ff % MXU_DIM == 0 else d_ff
    return row_tile, ff_tile


def _ffn_call(x, gain, wg, wu, wd, proj=None):
    n, d = x.shape
    d_ff = wg.shape[1]
    row_tile, ff_tile = _ffn_tiles(n, d_ff)
    row_spec = pl.BlockSpec((row_tile, d), lambda i: (i, 0))
    weight_specs = [_const_spec((1, d)), _const_spec((d, d_ff)), _const_spec((d, d_ff)),
                    _const_spec((d_ff, d))]
    if proj is None:
        body = functools.partial(_ffn_kernel, ff_tile=ff_tile)
        in_specs = [row_spec] + weight_specs
        args = (x, gain, wg, wu, wd)
        name = "ffn1"
    else:
        o_sb, o_rw, w_out = proj
        body = functools.partial(_proj_ffn_kernel, ff_tile=ff_tile)
        in_specs = [row_spec,
                    pl.BlockSpec((row_tile, SB_WIDTH), lambda i: (i, 0)),
                    pl.BlockSpec((row_tile, RW_WIDTH), lambda i: (i, 0)),
                    _const_spec(w_out.shape)] + weight_specs
        args = (x, o_sb, o_rw, w_out, gain, wg, wu, wd)
        name = "ffn2"
    return pl.pallas_call(
        body,
        out_shape=jax.ShapeDtypeStruct((n, d), F32),
        grid=(n // row_tile,),
        in_specs=in_specs,
        out_specs=row_spec,
        compiler_params=pltpu.CompilerParams(
            dimension_semantics=("parallel",), vmem_limit_bytes=VMEM_LIMIT),
        name=name,
    )(*args)


def _mix_in_kernel(x_ref, gain_ref, w_ref, qg_ref, kg_ref, mean_bd_ref,
                   q_ref, k_ref, vt_ref, rw_ref):
    h = _rms_norm_rows(x_ref[...], gain_ref[...]).astype(BF16)
    mean_bd = mean_bd_ref[...]

    def head_norm(t, gain):
        ms = _head_reduce(t * t, mean_bd, two_pass=False)
        return t * lax.rsqrt(ms + RMS_EPS) * gain

    q = jnp.dot(h, w_ref[:, 0:SB_WIDTH], preferred_element_type=F32)
    q_ref[...] = (head_norm(q, qg_ref[...]) * (HEAD_DIM ** -0.5)).astype(BF16)
    k = jnp.dot(h, w_ref[:, SB_WIDTH:2 * SB_WIDTH], preferred_element_type=F32)
    k_ref[...] = head_norm(k, kg_ref[...]).astype(BF16)
    vt = lax.dot_general(w_ref[:, 2 * SB_WIDTH:3 * SB_WIDTH], h, (((0,), (1,)), ((), ())),
                         preferred_element_type=F32).astype(BF16)
    for t in range(vt_ref.shape[0]):
        vt_ref[t] = vt[:, t * SB_TILE:(t + 1) * SB_TILE]
    rw_ref[...] = jnp.dot(h, w_ref[:, 3 * SB_WIDTH:], preferred_element_type=F32)


def _mix_in_call(x, gain, w_in, q_gain, k_gain, mean_bd):
    n, d = x.shape
    row_tile = 2 * SB_TILE
    assert n % row_tile == 0
    row = lambda width: pl.BlockSpec((row_tile, width), lambda i: (i, 0))
    return pl.pallas_call(
        _mix_in_kernel,
        out_shape=(jax.ShapeDtypeStruct((n, SB_WIDTH), BF16),
                   jax.ShapeDtypeStruct((n, SB_WIDTH), BF16),
                   jax.ShapeDtypeStruct((n // SB_TILE, SB_WIDTH, SB_TILE), BF16),
                   jax.ShapeDtypeStruct((n, RW_PROJ), F32)),
        grid=(n // row_tile,),
        in_specs=[row(d), _const_spec((1, d)), _const_spec(w_in.shape),
                  _const_spec((1, SB_WIDTH)), _const_spec((1, SB_WIDTH)),
                  _const_spec(mean_bd.shape)],
        out_specs=(row(SB_WIDTH), row(SB_WIDTH),
                   pl.BlockSpec((row_tile // SB_TILE, SB_WIDTH, SB_TILE), lambda i: (i, 0, 0)),
                   row(RW_PROJ)),
        compiler_params=pltpu.CompilerParams(
            dimension_semantics=("parallel",), vmem_limit_bytes=VMEM_LIMIT),
        name="mix_in",
    )(x, gain, w_in, q_gain, k_gain, mean_bd)


def _sb_attn_kernel(q_ref, k_ref, vt_ref, gain_ref, o_ref, *scratch):
    def query_tile(tile, _):
        rows = pl.ds(pl.multiple_of(tile * SB_QTILE, SB_QTILE), SB_QTILE)
        _sb_attn_tile(tile, q_ref.at[rows, :], k_ref, vt_ref, gain_ref, o_ref.at[rows, :],
                      *scratch)
        return 0

    lax.fori_loop(0, q_ref.shape[0] // SB_QTILE, query_tile, 0)


def _sb_attn_tile(tile, q_ref, k_ref, vt_ref, gain_ref, o_ref, later_ref, acc_ref):
    n_heads = q_ref.shape[1] // HEAD_DIM
    first_head = lax.broadcasted_iota(jnp.int32, (LANES, SB_QTILE), 0) < HEAD_DIM
    q_pairs = []
    for p in range(n_heads // 2):
        q_t = q_ref[:, p * LANES:(p + 1) * LANES].astype(F32).T
        q_pairs.append(jnp.concatenate([jnp.where(first_head, q_t, 0.0),
                                        jnp.where(first_head, 0.0, q_t)], axis=1).astype(BF16))
    kk = lax.broadcasted_iota(jnp.int32, (SB_TILE, SB_TILE), 0)
    jj = lax.broadcasted_iota(jnp.int32, (SB_TILE, SB_TILE), 1)
    later_keys = jnp.where(jj > kk, 1.0, 0.0).astype(BF16)

    def key_tile(j, q_start, n_queries, masked):
        k_t = k_ref[pl.ds(pl.multiple_of(j * SB_TILE, SB_TILE), SB_TILE), :]
        v_t = vt_ref[j]
        q_cols = slice(q_start, q_start + n_queries)
        if masked:
            causal = (lax.broadcasted_iota(jnp.int32, (SB_TILE, n_queries), 0)
                      < lax.broadcasted_iota(jnp.int32, (SB_TILE, n_queries), 1))

        def logits(p):
            q_pair = jnp.concatenate([q_pairs[p][:, q_cols],
                                      q_pairs[p][:, SB_QTILE + q_start:SB_QTILE + q_start + n_queries]],
                                     axis=1)
            return jnp.dot(k_t[:, p * LANES:(p + 1) * LANES], q_pair, preferred_element_type=F32)

        def finish(h, log_beta, sums, key0_row):
            rows = slice(h * HEAD_DIM, (h + 1) * HEAD_DIM)
            cols = slice(h * SB_QTILE + q_start, h * SB_QTILE + q_start + n_queries)
            later = later_ref[:, cols]
            att = jnp.exp(log_beta + sums + later).astype(BF16)
            acc_ref[rows, q_cols] += jnp.dot(v_t[rows, :], att, preferred_element_type=F32)
            later_ref[:, cols] = later + sums[0:1, :] + key0_row

        z_pair = logits(0)
        pending = None
        for h in range(n_heads):
            if h % 2 == 0:
                z_now = z_pair
                if h + 2 < n_heads:
                    z_pair = logits(h // 2 + 1)
            z = z_now[:, (h % 2) * n_queries:(h % 2 + 1) * n_queries]
            neg_part = jnp.minimum(z, 0.0)
            neg_relu = neg_part - z
            soft = jnp.log(1.0 + jnp.exp(neg_part + neg_relu))
            log_beta = neg_part - soft
            log_1m_beta = neg_relu - soft
            if masked:
                log_beta = jnp.where(causal, log_beta, MASKED_LOG_BETA)
                log_1m_beta = jnp.where(causal, log_1m_beta, 0.0)
            sums = jnp.dot(later_keys, log_1m_beta.astype(BF16), preferred_element_type=F32)
            if pending is not None:
                finish(*pending)
            pending = (h, log_beta, sums, log_1m_beta[0:1, :])
        finish(*pending)

    later_ref[...] = jnp.zeros_like(later_ref)
    acc_ref[...] = jnp.zeros_like(acc_ref)
    ratio = SB_QTILE // SB_TILE
    for d in reversed(range(ratio)):
        key_tile(ratio * tile + d, d * SB_TILE, SB_QTILE - d * SB_TILE, masked=True)

    def earlier_tile(i, _):
        key_tile(ratio * tile - 1 - i, 0, SB_QTILE, masked=False)
        return 0

    lax.fori_loop(0, ratio * tile, earlier_tile, 0)

    def head_norm(acc):
        ms = jnp.sum(acc * acc, axis=0, keepdims=True) * (1.0 / HEAD_DIM)
        return acc * lax.rsqrt(ms + RMS_EPS)

    out_t = jnp.concatenate([head_norm(acc_ref[h * HEAD_DIM:(h + 1) * HEAD_DIM, :])
                             for h in range(n_heads)], axis=0)
    o_ref[...] = (out_t.T * gain_ref[...]).astype(o_ref.dtype)


def _sb_attn_call(q, k, v_t, out_gain):
    b, s, width = q.shape
    step_lanes = SB_HEADS_PER_STEP * HEAD_DIM
    assert s % SB_QTILE == 0 and width % step_lanes == 0
    seq_spec = pl.BlockSpec((None, s, step_lanes), lambda bi, p: (bi, 0, p))
    all_heads = SB_HEADS_PER_STEP * SB_QTILE
    return pl.pallas_call(
        _sb_attn_kernel,
        out_shape=jax.ShapeDtypeStruct((b, s, width), BF16),
        grid=(b, width // step_lanes),
        in_specs=[seq_spec, seq_spec,
                  pl.BlockSpec((s // SB_TILE, step_lanes, SB_TILE), lambda bi, p: (bi, p, 0)),
                  pl.BlockSpec((1, step_lanes), lambda bi, p: (0, p))],
        out_specs=seq_spec,
        scratch_shapes=[pltpu.VMEM((1, all_heads), F32),
                        pltpu.VMEM((step_lanes, SB_QTILE), F32)],
        compiler_params=pltpu.CompilerParams(
            dimension_semantics=("parallel", "parallel"), vmem_limit_bytes=VMEM_LIMIT),
        name="sb_attn",
    )(q, k, v_t, out_gain)


def _rwkv_kernel(u_ref, mu_ref, w0_ref, w2_ref, a0_ref, a2_ref, g2_ref, kk_ref, ka_ref,
                 rk_ref, lnw_ref, lnb_ref, ones_bd_ref, tri_ref, o_ref,
                 prev_ref, state_ref, r_s, kf_s, v_s, kn_s, eta_s, ld_s, y_s):
    n_rows = u_ref.shape[0]

    @pl.when(pl.program_id(1) == 0)
    def _():
        prev_ref[...] = jnp.zeros_like(prev_ref)
        state_ref[...] = jnp.zeros_like(state_ref)

    u = u_ref[...]
    row_id = lax.broadcasted_iota(jnp.int32, u.shape, 0)
    shifted = jnp.where(row_id == 0, prev_ref[...], pltpu.roll(u, 1, axis=0))
    prev_ref[...] = u[n_rows - 1:n_rows, :]
    u = u + (shifted - u) * mu_ref[...]

    x_r = u[:, 0:RW_WIDTH]
    x_k = u[:, RW_WIDTH:2 * RW_WIDTH]
    x_v = u[:, 2 * RW_WIDTH:3 * RW_WIDTH]
    x_wa = u[:, 3 * RW_WIDTH:3 * RW_WIDTH + DECAY_LORA + AAA_LORA]
    x_g = u[:, 3 * RW_WIDTH + DECAY_LORA + AAA_LORA:]
    ones_bd = ones_bd_ref[...]

    pre = -(w0_ref[...] + _dot_split(jnp.tanh(x_wa), w2_ref[0], w2_ref[1]))
    log_w = -(jnp.maximum(pre, 0.0) + jnp.log(1.0 + jnp.exp(-jnp.abs(pre)))) - 0.5
    ld_s[...] = -jnp.exp(log_w)
    eta = jax.nn.sigmoid(a0_ref[...] + _dot(x_wa, a2_ref[...]))
    gate = _dot(jax.nn.sigmoid(x_g), g2_ref[...])
    kk = x_k * kk_ref[...]
    kk_scale = jnp.minimum(lax.rsqrt(_head_reduce(kk * kk, ones_bd, two_pass=True)),
                           1.0 / KK_NORM_FLOOR)
    kf = x_k * (1.0 + (eta - 1.0) * ka_ref[...])
    bonus = _head_reduce(x_r * kf * rk_ref[...], ones_bd, two_pass=False) * x_v
    r_s[...] = x_r
    kf_s[...] = kf
    v_s[...] = x_v
    kn_s[...] = kk * kk_scale
    eta_s[...] = eta

    lane = lax.broadcasted_iota(jnp.int32, (RW_CHUNK, LANES), 1)
    first_head = lane < HEAD_DIM
    ri = lax.broadcasted_iota(jnp.int32, (LANES, LANES), 0)
    ci = lax.broadcasted_iota(jnp.int32, (LANES, LANES), 1)
    same_head = (ri < HEAD_DIM) == (ci < HEAD_DIM)
    strict_lower = same_head & (ci < ri)
    lower = same_head & (ci <= ri)
    eye = (ri == ci).astype(F32)
    tri = tri_ref[...]

    def stack_heads(t):
        return jnp.concatenate([jnp.where(first_head, t, 0.0), jnp.where(first_head, 0.0, t)],
                               axis=0)

    n_pairs = RW_WIDTH // LANES

    def chunk_operands(rows):
        ld = ld_s[rows, :]
        cum = _prefix_sum(tri, ld)
        total = cum[RW_CHUNK - 1:RW_CHUNK, :]
        e_incl = jnp.exp(cum)
        e_inv = jnp.exp(-cum)
        e_tail = jnp.exp(total - cum)
        w_total = jnp.exp(total)
        kn = kn_s[rows, :]
        kf_c = kf_s[rows, :]
        b = kn * eta_s[rows, :]
        a_dec = -kn * jnp.exp(cum - ld)
        r_dec = r_s[rows, :] * e_incl
        b_inv = b * e_inv
        k_inv = kf_c * e_inv
        b_tail = b * e_tail
        k_tail = kf_c * e_tail
        v_c = v_s[rows, :]
        units = []
        for p in range(n_pairs):
            cols = slice(p * LANES, (p + 1) * LANES)
            units.append(dict(
                a_sm=stack_heads(a_dec[:, cols]), r_sm=stack_heads(r_dec[:, cols]),
                v_sm=stack_heads(v_c[:, cols]), b_inv=b_inv[:, cols], k_inv=k_inv[:, cols],
                b_tail_t=stack_heads(b_tail[:, cols]).T, k_tail_t=stack_heads(k_tail[:, cols]).T,
                decay_diag=eye * w_total[:, cols]))
        return units

    def chunk_group(g, _):
        row_slices = [pl.ds(pl.multiple_of((g * RW_GROUP + c) * RW_CHUNK, RW_CHUNK), RW_CHUNK)
                      for c in range(RW_GROUP)]
        units = [u for rows in row_slices for u in chunk_operands(rows)]
        scores = [_dot_nt(jnp.concatenate([u["a_sm"], u["r_sm"]], axis=0),
                          jnp.concatenate([u["b_inv"], u["k_inv"]], axis=0))
                  for u in units]

        def block_diag(s, mask):
            swapped = pltpu.roll(s, RW_CHUNK, axis=1)
            x_bd = jnp.concatenate([s[:RW_CHUNK], swapped[RW_CHUNK:]], axis=0)
            y_bd = jnp.concatenate([swapped[:RW_CHUNK], s[RW_CHUNK:]], axis=0)
            return jnp.where(mask, x_bd, 0.0), jnp.where(mask, y_bd, 0.0)

        l_ab, l_ak = zip(*[block_diag(s[:LANES], strict_lower) for s in scores])
        m_rb, m_rk = zip(*[block_diag(s[LANES:], lower) for s in scores])
        lakv = [_dot(l, u["v_sm"]) for l, u in zip(l_ak, units)]
        inv = [eye + l for l in l_ab]
        power = [_dot(l, l) for l in l_ab]
        for _ in range(4):
            both = [_dot(pw, jnp.concatenate([pw, iv], axis=1)) for pw, iv in zip(power, inv)]
            inv = [iv + x[:, LANES:] for iv, x in zip(inv, both)]
            power = [x[:, :LANES] for x in both]
        inv = [iv + _dot(pw, iv) for iv, pw in zip(inv, power)]
        pu = [_dot(iv, jnp.concatenate([u["a_sm"], lv], axis=1))
              for iv, u, lv in zip(inv, units, lakv)]
        zeros = jnp.zeros((LANES, LANES), F32)
        tail = [_dot(jnp.concatenate([jnp.concatenate([mb, mk], axis=1),
                                      jnp.concatenate([u["b_tail_t"], u["k_tail_t"]], axis=1)],
                                     axis=0),
                     jnp.concatenate([x, jnp.concatenate([zeros, u["v_sm"]], axis=1)], axis=0))
                for mb, mk, u, x in zip(m_rb, m_rk, units, pu)]
        states = [state_ref[p] for p in range(n_pairs)]
        for c, rows in enumerate(row_slices):
            new_states = []
            for p in range(n_pairs):
                i = c * n_pairs + p
                q_sm = units[i]["r_sm"] + tail[i][:LANES, :LANES]
                y_sm = _dot(q_sm, states[p]) + tail[i][:LANES, LANES:]
                g_bd = tail[i][LANES:, :LANES] + units[i]["decay_diag"]
                new_states.append(_dot(g_bd, states[p]) + tail[i][LANES:, LANES:])
                y_s[rows, p * LANES:(p + 1) * LANES] = y_sm[:RW_CHUNK, :] + y_sm[RW_CHUNK:, :]
            states = new_states
        for p in range(n_pairs):
            state_ref[p] = states[p]
        return 0

    lax.fori_loop(0, n_rows // (RW_CHUNK * RW_GROUP), chunk_group, 0)

    y = y_s[...]
    centered = y - _head_reduce(y, ones_bd, two_pass=False) * (1.0 / HEAD_DIM)
    var = _head_reduce(centered * centered, ones_bd, two_pass=False) * (1.0 / HEAD_DIM)
    yn = centered * lax.rsqrt(var + LNX_EPS) * lnw_ref[...] + lnb_ref[...]
    o_ref[...] = ((yn + bonus) * gate).astype(o_ref.dtype)


def _rwkv_call(rw_in, mu, w0, w2_pad, a0, a2_pad, g2, k_k, k_a, r_k, ln_w, ln_b, ones_bd, tri):
    b, s, width = rw_in.shape
    row_tile = next(t for t in (512, 256, 128) if s % t == 0)
    vec = lambda n: _const_spec((1, n))
    scratch_rows = pltpu.VMEM((row_tile, RW_WIDTH), F32)
    return pl.pallas_call(
        _rwkv_kernel,
        out_shape=jax.ShapeDtypeStruct((b, s, RW_WIDTH), BF16),
        grid=(b, s // row_tile),
        in_specs=[pl.BlockSpec((None, row_tile, width), lambda bi, j: (bi, j, 0)),
                  vec(width), vec(RW_WIDTH), _const_spec(w2_pad.shape), vec(RW_WIDTH),
                  _const_spec(a2_pad.shape), _const_spec(g2.shape), vec(RW_WIDTH),
                  vec(RW_WIDTH), vec(RW_WIDTH), vec(RW_WIDTH), vec(RW_WIDTH),
                  _const_spec(ones_bd.shape), _const_spec(tri.shape)],
        out_specs=pl.BlockSpec((None, row_tile, RW_WIDTH), lambda bi, j: (bi, j, 0)),
        scratch_shapes=[pltpu.VMEM((1, width), F32),
                        pltpu.VMEM((RW_WIDTH // LANES, LANES, LANES), F32)]
                       + [scratch_rows] * 7,
        compiler_params=pltpu.CompilerParams(
            dimension_semantics=("parallel", "arbitrary"), vmem_limit_bytes=VMEM_LIMIT),
        name="rwkv7",
    )(rw_in, mu, w0, w2_pad, a0, a2_pad, g2, k_k, k_a, r_k, ln_w, ln_b, ones_bd, tri)


def _head_block_diag(value, dtype):
    idx = jnp.arange(MXU_DIM) // HEAD_DIM
    return jnp.where(idx[:, None] == idx[None, :], value, 0.0).astype(dtype)


def kernel(x, norm_ffn1, ffn1_gate, ffn1_up, ffn1_down, norm_mix, w_in, sb_q_norm, sb_k_norm,
           sb_out_norm, rw_mu, rw_w0, rw_w2, rw_a0, rw_a2, rw_g2, rw_k_k, rw_k_a, rw_r_k,
           rw_ln_w, rw_ln_b, w_out, norm_ffn2, ffn2_gate, ffn2_up, ffn2_down):
    b, s, d = x.shape
    n = b * s
    heads = SB_WIDTH // HEAD_DIM
    mean_bd = _head_block_diag(1.0 / HEAD_DIM, BF16)
    ones_bd = _head_block_diag(1.0, BF16)
    tri = jnp.tril(jnp.ones((RW_CHUNK, RW_CHUNK), BF16))
    row = lambda t: t.reshape(1, -1)

    h = x.reshape(n, d)
    for l in range(norm_ffn1.shape[0]):
        h = _ffn_call(h, row(norm_ffn1[l]), ffn1_gate[l].astype(BF16), ffn1_up[l].astype(BF16),
                      ffn1_down[l].astype(BF16))

        q, k, v_t, rw_in = _mix_in_call(
            h, row(norm_mix[l]), w_in[l].astype(BF16),
            row(jnp.tile(sb_q_norm[l], heads)), row(jnp.tile(sb_k_norm[l], heads)), mean_bd)

        o_sb = _sb_attn_call(q.reshape(b, s, SB_WIDTH), k.reshape(b, s, SB_WIDTH), v_t,
                             row(sb_out_norm[l]))

        zeros = jnp.zeros((DECAY_LORA, RW_WIDTH), F32)
        w2_pad = jnp.stack(_split_bf16(jnp.concatenate([rw_w2[l], zeros], axis=0)))
        a2_pad = jnp.concatenate([zeros, rw_a2[l]], axis=0).astype(BF16)
        o_rw = _rwkv_call(rw_in.reshape(b, s, RW_PROJ), row(rw_mu[l]), row(rw_w0[l]), w2_pad,
                          row(rw_a0[l]), a2_pad, rw_g2[l].astype(BF16), row(rw_k_k[l]),
                          row(rw_k_a[l]), row(rw_r_k[l]), row(rw_ln_w[l]), row(rw_ln_b[l]),
                          ones_bd, tri)

        h = _ffn_call(h, row(norm_ffn2[l]), ffn2_gate[l].astype(BF16), ffn2_up[l].astype(BF16),
                      ffn2_down[l].astype(BF16),
                      proj=(o_sb.reshape(n, SB_WIDTH), o_rw.reshape(n, RW_WIDTH),
                            w_out[l].astype(BF16)))
    return h.reshape(b, s, d)
```

```python
import functools

import jax
import jax.numpy as jnp
from jax import lax
from jax.experimental import pallas as pl
from jax.experimental.pallas import tpu as pltpu

F32 = jnp.float32
BF16 = jnp.bfloat16

HEAD_DIM = 64
SB_WIDTH = 512
RW_WIDTH = 512
DECAY_LORA = 64
AAA_LORA = 64
GATE_LORA = 128
RW_PROJ = 3 * RW_WIDTH + DECAY_LORA + AAA_LORA + GATE_LORA
RMS_EPS = 1e-6
LNX_EPS = 64e-5
KK_NORM_FLOOR = 1e-12

LANES = 128
MXU_DIM = 256
RW_CHUNK = 64
RW_GROUP = 4
SB_TILE = 256
SB_QTILE = 2 * SB_TILE
SB_HEADS_PER_STEP = 8
MASKED_LOG_BETA = -1e30
V7X_VMEM_BYTES = 64 * 1024 * 1024
VMEM_LIMIT = V7X_VMEM_BYTES * 3 // 4


def _const_spec(shape):
    zeros = (0,) * len(shape)
    return pl.BlockSpec(shape, lambda *_: zeros, pipeline_mode=pl.Buffered(1))


def _dot(a, b):
    return jnp.dot(a.astype(BF16), b.astype(BF16), preferred_element_type=F32)


def _dot_nt(a, b):
    return lax.dot_general(a.astype(BF16), b.astype(BF16), (((1,), (1,)), ((), ())),
                           preferred_element_type=F32)


def _split_bf16(x):
    hi = x.astype(BF16)
    return hi, (x - hi.astype(F32)).astype(BF16)


def _dot_split(a, b_hi, b_lo):
    a_hi, a_lo = _split_bf16(a)
    return (jnp.dot(a_hi, b_hi, preferred_element_type=F32)
            + jnp.dot(a_hi, b_lo, preferred_element_type=F32)
            + jnp.dot(a_lo, b_hi, preferred_element_type=F32))


def _prefix_sum(tri, x):
    hi = x.astype(BF16)
    rest = x - hi.astype(F32)
    mid = rest.astype(BF16)
    lo = (rest - mid.astype(F32)).astype(BF16)
    return (jnp.dot(tri, hi, preferred_element_type=F32)
            + jnp.dot(tri, mid, preferred_element_type=F32)
            + jnp.dot(tri, lo, preferred_element_type=F32))


def _rms_norm_rows(x, gain):
    ms = jnp.mean(x * x, axis=-1, keepdims=True)
    return x * lax.rsqrt(ms + RMS_EPS) * gain


def _head_reduce(t, bd, two_pass):
    outs = []
    for half in range(t.shape[1] // MXU_DIM):
        th = t[:, half * MXU_DIM:(half + 1) * MXU_DIM]
        hi = th.astype(BF16)
        red = jnp.dot(hi, bd, preferred_element_type=F32)
        if two_pass:
            lo = (th - hi.astype(F32)).astype(BF16)
            red = red + jnp.dot(lo, bd, preferred_element_type=F32)
        outs.append(red)
    return jnp.concatenate(outs, axis=1)


def _swiglu_residual(x, gain_ref, wg_ref, wu_ref, wd_ref, ff_tile):
    h = _rms_norm_rows(x, gain_ref[...]).astype(BF16)
    acc = x
    for start in range(0, wg_ref.shape[1], ff_tile):
        cols = slice(start, min(start + ff_tile, wg_ref.shape[1]))
        gate = jnp.dot(h, wg_ref[:, cols], preferred_element_type=F32)
        up = jnp.dot(h, wu_ref[:, cols], preferred_element_type=F32)
        act = (gate * jax.nn.sigmoid(gate) * up * 0.5).astype(BF16)
        acc = acc + jnp.dot(act, wd_ref[cols, :], preferred_element_type=F32)
    return acc


def _ffn_kernel(x_ref, gain_ref, wg_ref, wu_ref, wd_ref, o_ref, *, ff_tile):
    o_ref[...] = _swiglu_residual(x_ref[...], gain_ref, wg_ref, wu_ref, wd_ref, ff_tile)


def _proj_ffn_kernel(x_ref, osb_ref, orw_ref, wo_ref, gain_ref, wg_ref, wu_ref, wd_ref,
                     o_ref, *, ff_tile):
    mixed = (jnp.dot(osb_ref[...], wo_ref[:SB_WIDTH, :], preferred_element_type=F32)
             + jnp.dot(orw_ref[...], wo_ref[SB_WIDTH:, :], preferred_element_type=F32))
    x = mixed + x_ref[...]
    o_ref[...] = _swiglu_residual(x, gain_ref, wg_ref, wu_ref, wd_ref, ff_tile)


def _ffn_tiles(n_tokens, d_ff):
    row_tile = 512 if n_tokens % 512 == 0 else n_tokens
    ff_tile = -(-d_ff // (2 * MXU_DIM)) * MXU_DIM if d_ff % MXU_DIM == 0 else d_ff
    return row_tile, ff_tile


def _ffn_call(x, gain, wg, wu, wd, proj=None):
    n, d = x.shape
    d_ff = wg.shape[1]
    row_tile, ff_tile = _ffn_tiles(n, d_ff)
    row_spec = pl.BlockSpec((row_tile, d), lambda i: (i, 0))
    weight_specs = [_const_spec((1, d)), _const_spec((d, d_ff)), _const_spec((d, d_ff)),
                    _const_spec((d_ff, d))]
    if proj is None:
        body = functools.partial(_ffn_kernel, ff_tile=ff_tile)
        in_specs = [row_spec] + weight_specs
        args = (x, gain, wg, wu, wd)
        name = "ffn1"
    else:
        o_sb, o_rw, w_out = proj
        body = functools.partial(_proj_ffn_kernel, ff_tile=ff_tile)
        in_specs = [row_spec,
                    pl.BlockSpec((row_tile, SB_WIDTH), lambda i: (i, 0)),
                    pl.BlockSpec((row_tile, RW_WIDTH), lambda i: (i, 0)),
                    _const_spec(w_out.shape)] + weight_specs
        args = (x, o_sb, o_rw, w_out, gain, wg, wu, wd)
        name = "ffn2"
    return pl.pallas_call(
        body,
        out_shape=jax.ShapeDtypeStruct((n, d), F32),
        grid=(n // row_tile,),
        in_specs=in_specs,
        out_specs=row_spec,
        compiler_params=pltpu.CompilerParams(
            dimension_semantics=("parallel",), vmem_limit_bytes=VMEM_LIMIT),
        name=name,
    )(*args)


def _mix_in_kernel(x_ref, gain_ref, w_ref, qg_ref, kg_ref, mean_bd_ref,
                   q_ref, k_ref, vt_ref, rw_ref):
    h = _rms_norm_rows(x_ref[...], gain_ref[...]).astype(BF16)
    mean_bd = mean_bd_ref[...]

    def head_norm(t, gain):
        ms = _head_reduce(t * t, mean_bd, two_pass=False)
        return t * lax.rsqrt(ms + RMS_EPS) * gain

    q = jnp.dot(h, w_ref[:, 0:SB_WIDTH], preferred_element_type=F32)
    q_ref[...] = (head_norm(q, qg_ref[...]) * (HEAD_DIM ** -0.5)).astype(BF16)
    k = jnp.dot(h, w_ref[:, SB_WIDTH:2 * SB_WIDTH], preferred_element_type=F32)
    k_ref[...] = head_norm(k, kg_ref[...]).astype(BF16)
    vt = lax.dot_general(w_ref[:, 2 * SB_WIDTH:3 * SB_WIDTH], h, (((0,), (1,)), ((), ())),
                         preferred_element_type=F32).astype(BF16)
    for t in range(vt_ref.shape[0]):
        vt_ref[t] = vt[:, t * SB_TILE:(t + 1) * SB_TILE]
    rw_ref[...] = jnp.dot(h, w_ref[:, 3 * SB_WIDTH:], preferred_element_type=F32)


def _mix_in_call(x, gain, w_in, q_gain, k_gain, mean_bd):
    n, d = x.shape
    row_tile = 2 * SB_TILE
    assert n % row_tile == 0
    row = lambda width: pl.BlockSpec((row_tile, width), lambda i: (i, 0))
    return pl.pallas_call(
        _mix_in_kernel,
        out_shape=(jax.ShapeDtypeStruct((n, SB_WIDTH), BF16),
                   jax.ShapeDtypeStruct((n, SB_WIDTH), BF16),
                   jax.ShapeDtypeStruct((n // SB_TILE, SB_WIDTH, SB_TILE), BF16),
                   jax.ShapeDtypeStruct((n, RW_PROJ), F32)),
        grid=(n // row_tile,),
        in_specs=[row(d), _const_spec((1, d)), _const_spec(w_in.shape),
                  _const_spec((1, SB_WIDTH)), _const_spec((1, SB_WIDTH)),
                  _const_spec(mean_bd.shape)],
        out_specs=(row(SB_WIDTH), row(SB_WIDTH),
                   pl.BlockSpec((row_tile // SB_TILE, SB_WIDTH, SB_TILE), lambda i: (i, 0, 0)),
                   row(RW_PROJ)),
        compiler_params=pltpu.CompilerParams(
            dimension_semantics=("parallel",), vmem_limit_bytes=VMEM_LIMIT),
        name="mix_in",
    )(x, gain, w_in, q_gain, k_gain, mean_bd)


def _sb_attn_kernel(q_ref, k_ref, vt_ref, gain_ref, o_ref, *scratch):
    def query_tile(tile, _):
        rows = pl.ds(pl.multiple_of(tile * SB_QTILE, SB_QTILE), SB_QTILE)
        _sb_attn_tile(tile, q_ref.at[rows, :], k_ref, vt_ref, gain_ref, o_ref.at[rows, :],
                      *scratch)
        return 0

    lax.fori_loop(0, q_ref.shape[0] // SB_QTILE, query_tile, 0)


def _sb_attn_tile(tile, q_ref, k_ref, vt_ref, gain_ref, o_ref, later_ref, acc_ref):
    n_heads = q_ref.shape[1] // HEAD_DIM
    first_head = lax.broadcasted_iota(jnp.int32, (LANES, SB_QTILE), 0) < HEAD_DIM
    q_pairs = []
    for p in range(n_heads // 2):
        q_t = q_ref[:, p * LANES:(p + 1) * LANES].astype(F32).T
        q_pairs.append(jnp.concatenate([jnp.where(first_head, q_t, 0.0),
                                        jnp.where(first_head, 0.0, q_t)], axis=1).astype(BF16))
    kk = lax.broadcasted_iota(jnp.int32, (SB_TILE, SB_TILE), 0)
    jj = lax.broadcasted_iota(jnp.int32, (SB_TILE, SB_TILE), 1)
    later_keys = jnp.where(jj > kk, 1.0, 0.0).astype(BF16)

    def key_tile(j, q_start, n_queries, masked):
        k_t = k_ref[pl.ds(pl.multiple_of(j * SB_TILE, SB_TILE), SB_TILE), :]
        v_t = vt_ref[j]
        q_cols = slice(q_start, q_start + n_queries)
        if masked:
            causal = (lax.broadcasted_iota(jnp.int32, (SB_TILE, n_queries), 0)
                      < lax.broadcasted_iota(jnp.int32, (SB_TILE, n_queries), 1))

        def logits(p):
            q_pair = jnp.concatenate([q_pairs[p][:, q_cols],
                                      q_pairs[p][:, SB_QTILE + q_start:SB_QTILE + q_start + n_queries]],
                                     axis=1)
            return jnp.dot(k_t[:, p * LANES:(p + 1) * LANES], q_pair, preferred_element_type=F32)

        def finish(h, log_beta, sums, key0_row):
            rows = slice(h * HEAD_DIM, (h + 1) * HEAD_DIM)
            cols = slice(h * SB_QTILE + q_start, h * SB_QTILE + q_start + n_queries)
            later = later_ref[:, cols]
            att = jnp.exp(log_beta + sums + later).astype(BF16)
            acc_ref[rows, q_cols] += jnp.dot(v_t[rows, :], att, preferred_element_type=F32)
            later_ref[:, cols] = later + sums[0:1, :] + key0_row

        z_pair = logits(0)
        pending = None
        for h in range(n_heads):
            if h % 2 == 0:
                z_now = z_pair
                if h + 2 < n_heads:
                    z_pair = logits(h // 2 + 1)
            z = z_now[:, (h % 2) * n_queries:(h % 2 + 1) * n_queries]
            neg_part = jnp.minimum(z, 0.0)
            neg_relu = neg_part - z
            soft = jnp.log(1.0 + jnp.exp(neg_part + neg_relu))
            log_beta = neg_part - soft
            log_1m_beta = neg_relu - soft
            if masked:
                log_beta = jnp.where(causal, log_beta, MASKED_LOG_BETA)
                log_1m_beta = jnp.where(causal, log_1m_beta, 0.0)
            sums = jnp.dot(later_keys, log_1m_beta.astype(BF16), preferred_element_type=F32)
            if pending is not None:
                finish(*pending)
            pending = (h, log_beta, sums, log_1m_beta[0:1, :])
        finish(*pending)

    later_ref[...] = jnp.zeros_like(later_ref)
    acc_ref[...] = jnp.zeros_like(acc_ref)
    ratio = SB_QTILE // SB_TILE
    for d in reversed(range(ratio)):
        key_tile(ratio * tile + d, d * SB_TILE, SB_QTILE - d * SB_TILE, masked=True)

    def earlier_tile(i, _):
        key_tile(ratio * tile - 1 - i, 0, SB_QTILE, masked=False)
        return 0

    lax.fori_loop(0, ratio * tile, earlier_tile, 0)

    def head_norm(acc):
        ms = jnp.sum(acc * acc, axis=0, keepdims=True) * (1.0 / HEAD_DIM)
        return acc * lax.rsqrt(ms + RMS_EPS)

    out_t = jnp.concatenate([head_norm(acc_ref[h * HEAD_DIM:(h + 1) * HEAD_DIM, :])
                             for h in range(n_heads)], axis=0)
    o_ref[...] = (out_t.T * gain_ref[...]).astype(o_ref.dtype)


def _sb_attn_call(q, k, v_t, out_gain):
    b, s, width = q.shape
    step_lanes = SB_HEADS_PER_STEP * HEAD_DIM
    assert s % SB_QTILE == 0 and width % step_lanes == 0
    seq_spec = pl.BlockSpec((None, s, step_lanes), lambda bi, p: (bi, 0, p))
    all_heads = SB_HEADS_PER_STEP * SB_QTILE
    return pl.pallas_call(
        _sb_attn_kernel,
        out_shape=jax.ShapeDtypeStruct((b, s, width), BF16),
        grid=(b, width // step_lanes),
        in_specs=[seq_spec, seq_spec,
                  pl.BlockSpec((s // SB_TILE, step_lanes, SB_TILE), lambda bi, p: (bi, p, 0)),
                  pl.BlockSpec((1, step_lanes), lambda bi, p: (0, p))],
        out_specs=seq_spec,
        scratch_shapes=[pltpu.VMEM((1, all_heads), F32),
                        pltpu.VMEM((step_lanes, SB_QTILE), F32)],
        compiler_params=pltpu.CompilerParams(
            dimension_semantics=("parallel", "parallel"), vmem_limit_bytes=VMEM_LIMIT),
        name="sb_attn",
    )(q, k, v_t, out_gain)


def _rwkv_kernel(u_ref, mu_ref, w0_ref, w2_ref, a0_ref, a2_ref, g2_ref, kk_ref, ka_ref,
                 rk_ref, lnw_ref, lnb_ref, ones_bd_ref, tri_ref, o_ref,
                 prev_ref, state_ref, r_s, kf_s, v_s, kn_s, eta_s, ld_s, y_s):
    n_rows = u_ref.shape[0]

    @pl.when(pl.program_id(1) == 0)
    def _():
        prev_ref[...] = jnp.zeros_like(prev_ref)
        state_ref[...] = jnp.zeros_like(state_ref)

    u = u_ref[...]
    row_id = lax.broadcasted_iota(jnp.int32, u.shape, 0)
    shifted = jnp.where(row_id == 0, prev_ref[...], pltpu.roll(u, 1, axis=0))
    prev_ref[...] = u[n_rows - 1:n_rows, :]
    u = u + (shifted - u) * mu_ref[...]

    x_r = u[:, 0:RW_WIDTH]
    x_k = u[:, RW_WIDTH:2 * RW_WIDTH]
    x_v = u[:, 2 * RW_WIDTH:3 * RW_WIDTH]
    x_wa = u[:, 3 * RW_WIDTH:3 * RW_WIDTH + DECAY_LORA + AAA_LORA]
    x_g = u[:, 3 * RW_WIDTH + DECAY_LORA + AAA_LORA:]
    ones_bd = ones_bd_ref[...]

    pre = -(w0_ref[...] + _dot_split(jnp.tanh(x_wa), w2_ref[0], w2_ref[1]))
    log_w = -(jnp.maximum(pre, 0.0) + jnp.log(1.0 + jnp.exp(-jnp.abs(pre)))) - 0.5
    ld_s[...] = -jnp.exp(log_w)
    eta = jax.nn.sigmoid(a0_ref[...] + _dot(x_wa, a2_ref[...]))
    gate = _dot(jax.nn.sigmoid(x_g), g2_ref[...])
    kk = x_k * kk_ref[...]
    kk_scale = jnp.minimum(lax.rsqrt(_head_reduce(kk * kk, ones_bd, two_pass=True)),
                           1.0 / KK_NORM_FLOOR)
    kf = x_k * (1.0 + (eta - 1.0) * ka_ref[...])
    bonus = _head_reduce(x_r * kf * rk_ref[...], ones_bd, two_pass=False) * x_v
    r_s[...] = x_r
    kf_s[...] = kf
    v_s[...] = x_v
    kn_s[...] = kk * kk_scale
    eta_s[...] = eta

    lane = lax.broadcasted_iota(jnp.int32, (RW_CHUNK, LANES), 1)
    first_head = lane < HEAD_DIM
    ri = lax.broadcasted_iota(jnp.int32, (LANES, LANES), 0)
    ci = lax.broadcasted_iota(jnp.int32, (LANES, LANES), 1)
    same_head = (ri < HEAD_DIM) == (ci < HEAD_DIM)
    strict_lower = same_head & (ci < ri)
    lower = same_head & (ci <= ri)
    eye = (ri == ci).astype(F32)
    tri = tri_ref[...]

    def stack_heads(t):
        return jnp.concatenate([jnp.where(first_head, t, 0.0), jnp.where(first_head, 0.0, t)],
                               axis=0)

    n_pairs = RW_WIDTH // LANES

    def chunk_operands(rows):
        ld = ld_s[rows, :]
        cum = _prefix_sum(tri, ld)
        total = cum[RW_CHUNK - 1:RW_CHUNK, :]
        e_incl = jnp.exp(cum)
        e_inv = jnp.exp(-cum)
        e_tail = jnp.exp(total - cum)
        w_total = jnp.exp(total)
        kn = kn_s[rows, :]
        kf_c = kf_s[rows, :]
        b = kn * eta_s[rows, :]
        a_dec = -kn * jnp.exp(cum - ld)
        r_dec = r_s[rows, :] * e_incl
        b_inv = b * e_inv
        k_inv = kf_c * e_inv
        b_tail = b * e_tail
        k_tail = kf_c * e_tail
        v_c = v_s[rows, :]
        units = []
        for p in range(n_pairs):
            cols = slice(p * LANES, (p + 1) * LANES)
            units.append(dict(
                a_sm=stack_heads(a_dec[:, cols]), r_sm=stack_heads(r_dec[:, cols]),
                v_sm=stack_heads(v_c[:, cols]), b_inv=b_inv[:, cols], k_inv=k_inv[:, cols],
                b_tail_t=stack_heads(b_tail[:, cols]).T, k_tail_t=stack_heads(k_tail[:, cols]).T,
                decay_diag=eye * w_total[:, cols]))
        return units

    def chunk_group(g, _):
        row_slices = [pl.ds(pl.multiple_of((g * RW_GROUP + c) * RW_CHUNK, RW_CHUNK), RW_CHUNK)
                      for c in range(RW_GROUP)]
        units = [u for rows in row_slices for u in chunk_operands(rows)]
        scores = [_dot_nt(jnp.concatenate([u["a_sm"], u["r_sm"]], axis=0),
                          jnp.concatenate([u["b_inv"], u["k_inv"]], axis=0))
                  for u in units]

        def block_diag(s, mask):
            swapped = pltpu.roll(s, RW_CHUNK, axis=1)
            x_bd = jnp.concatenate([s[:RW_CHUNK], swapped[RW_CHUNK:]], axis=0)
            y_bd = jnp.concatenate([swapped[:RW_CHUNK], s[RW_CHUNK:]], axis=0)
            return jnp.where(mask, x_bd, 0.0), jnp.where(mask, y_bd, 0.0)

        l_ab, l_ak = zip(*[block_diag(s[:LANES], strict_lower) for s in scores])
        m_rb, m_rk = zip(*[block_diag(s[LANES:], lower) for s in scores])
        lakv = [_dot(l, u["v_sm"]) for l, u in zip(l_ak, units)]
        inv = [eye + l for l in l_ab]
        power = [_dot(l, l) for l in l_ab]
        for _ in range(4):
            both = [_dot(pw, jnp.concatenate([pw, iv], axis=1)) for pw, iv in zip(power, inv)]
            inv = [iv + x[:, LANES:] for iv, x in zip(inv, both)]
            power = [x[:, :LANES] for x in both]
        inv = [iv + _dot(pw, iv) for iv, pw in zip(inv, power)]
        pu = [_dot(iv, jnp.concatenate([u["a_sm"], lv], axis=1))
              for iv, u, lv in zip(inv, units, lakv)]
        zeros = jnp.zeros((LANES, LANES), F32)
        tail = [_dot(jnp.concatenate([jnp.concatenate([mb, mk], axis=1),
                                      jnp.concatenate([u["b_tail_t"], u["k_tail_t"]], axis=1)],
                                     axis=0),
                     jnp.concatenate([x, jnp.concatenate([zeros, u["v_sm"]], axis=1)], axis=0))
                for mb, mk, u, x in zip(m_rb, m_rk, units, pu)]
        states = [state_ref[p] for p in range(n_pairs)]
        for c, rows in enumerate(row_slices):
            new_states = []
            for p in range(n_pairs):
                i = c * n_pairs + p
                q_sm = units[i]["r_sm"] + tail[i][:LANES, :LANES]
                y_sm = _dot(q_sm, states[p]) + tail[i][:LANES, LANES:]
                g_bd = tail[i][LANES:, :LANES] + units[i]["decay_diag"]
                new_states.append(_dot(g_bd, states[p]) + tail[i][LANES:, LANES:])
                y_s[rows, p * LANES:(p + 1) * LANES] = y_sm[:RW_CHUNK, :] + y_sm[RW_CHUNK:, :]
            states = new_states
        for p in range(n_pairs):
            state_ref[p] = states[p]
        return 0

    lax.fori_loop(0, n_rows // (RW_CHUNK * RW_GROUP), chunk_group, 0)

    y = y_s[...]
    centered = y - _head_reduce(y, ones_bd, two_pass=False) * (1.0 / HEAD_DIM)
    var = _head_reduce(centered * centered, ones_bd, two_pass=False) * (1.0 / HEAD_DIM)
    yn = centered * lax.rsqrt(var + LNX_EPS) * lnw_ref[...] + lnb_ref[...]
    o_ref[...] = ((yn + bonus) * gate).astype(o_ref.dtype)


def _rwkv_call(rw_in, mu, w0, w2_pad, a0, a2_pad, g2, k_k, k_a, r_k, ln_w, ln_b, ones_bd, tri):
    b, s, width = rw_in.shape
    row_tile = next(t for t in (1024, 512, 256, 128) if s % t == 0)
    vec = lambda n: _const_spec((1, n))
    scratch_rows = pltpu.VMEM((row_tile, RW_WIDTH), F32)
    return pl.pallas_call(
        _rwkv_kernel,
        out_shape=jax.ShapeDtypeStruct((b, s, RW_WIDTH), BF16),
        grid=(b, s // row_tile),
        in_specs=[pl.BlockSpec((None, row_tile, width), lambda bi, j: (bi, j, 0)),
                  vec(width), vec(RW_WIDTH), _const_spec(w2_pad.shape), vec(RW_WIDTH),
                  _const_spec(a2_pad.shape), _const_spec(g2.shape), vec(RW_WIDTH),
                  vec(RW_WIDTH), vec(RW_WIDTH), vec(RW_WIDTH), vec(RW_WIDTH),
                  _const_spec(ones_bd.shape), _const_spec(tri.shape)],
        out_specs=pl.BlockSpec((None, row_tile, RW_WIDTH), lambda bi, j: (bi, j, 0)),
        scratch_shapes=[pltpu.VMEM((1, width), F32),
                        pltpu.VMEM((RW_WIDTH // LANES, LANES, LANES), F32)]
                       + [scratch_rows] * 7,
        compiler_params=pltpu.CompilerParams(
            dimension_semantics=("parallel", "arbitrary"), vmem_limit_bytes=VMEM_LIMIT),
        name="rwkv7",
    )(rw_in, mu, w0, w2_pad, a0, a2_pad, g2, k_k, k_a, r_k, ln_w, ln_b, ones_bd, tri)


def _head_block_diag(value, dtype):
    idx = jnp.arange(MXU_DIM) // HEAD_DIM
    return jnp.where(idx[:, None] == idx[None, :], value, 0.0).astype(dtype)


def kernel(x, norm_ffn1, ffn1_gate, ffn1_up, ffn1_down, norm_mix, w_in, sb_q_norm, sb_k_norm,
           sb_out_norm, rw_mu, rw_w0, rw_w2, rw_a0, rw_a2, rw_g2, rw_k_k, rw_k_a, rw_r_k,
           rw_ln_w, rw_ln_b, w_out, norm_ffn2, ffn2_gate, ffn2_up, ffn2_down):
    b, s, d = x.shape
    n = b * s
    heads = SB_WIDTH // HEAD_DIM
    mean_bd = _head_block_diag(1.0 / HEAD_DIM, BF16)
    ones_bd = _head_block_diag(1.0, BF16)
    tri = jnp.tril(jnp.ones((RW_CHUNK, RW_CHUNK), BF16))
    row = lambda t: t.reshape(1, -1)

    h = x.reshape(n, d)
    for l in range(norm_ffn1.shape[0]):
        h = _ffn_call(h, row(norm_ffn1[l]), ffn1_gate[l].astype(BF16), ffn1_up[l].astype(BF16),
                      ffn1_down[l].astype(BF16))

        q, k, v_t, rw_in = _mix_in_call(
            h, row(norm_mix[l]), w_in[l].astype(BF16),
            row(jnp.tile(sb_q_norm[l], heads)), row(jnp.tile(sb_k_norm[l], heads)), mean_bd)

        o_sb = _sb_attn_call(q.reshape(b, s, SB_WIDTH), k.reshape(b, s, SB_WIDTH), v_t,
                             row(sb_out_norm[l]))

        zeros = jnp.zeros((DECAY_LORA, RW_WIDTH), F32)
        w2_pad = jnp.stack(_split_bf16(jnp.concatenate([rw_w2[l], zeros], axis=0)))
        a2_pad = jnp.concatenate([zeros, rw_a2[l]], axis=0).astype(BF16)
        o_rw = _rwkv_call(rw_in.reshape(b, s, RW_PROJ), row(rw_mu[l]), row(rw_w0[l]), w2_pad,
                          row(rw_a0[l]), a2_pad, rw_g2[l].astype(BF16), row(rw_k_k[l]),
                          row(rw_k_a[l]), row(rw_r_k[l]), row(rw_ln_w[l]), row(rw_ln_b[l]),
                          ones_bd, tri)

        h = _ffn_call(h, row(norm_ffn2[l]), ffn2_gate[l].astype(BF16), ffn2_up[l].astype(BF16),
                      ffn2_down[l].astype(BF16),
                      proj=(o_sb.reshape(n, SB_WIDTH), o_rw.reshape(n, RW_WIDTH),
                            w_out[l].astype(BF16)))
    return h.reshape(b, s, d)
```

```python
import functools

import jax
import jax.numpy as jnp
from jax import lax
from jax.experimental import pallas as pl
from jax.experimental.pallas import tpu as pltpu

F32 = jnp.float32
BF16 = jnp.bfloat16

HEAD_DIM = 64
SB_WIDTH = 512
RW_WIDTH = 512
DECAY_LORA = 64
AAA_LORA = 64
GATE_LORA = 128
RW_PROJ = 3 * RW_WIDTH + DECAY_LORA + AAA_LORA + GATE_LORA
RMS_EPS = 1e-6
LNX_EPS = 64e-5
KK_NORM_FLOOR = 1e-12

LANES = 128
MXU_DIM = 256
RW_CHUNK = 64
RW_GROUP = 4
SB_TILE = 256
SB_QTILE = 2 * SB_TILE
SB_HEADS_PER_STEP = 8
MASKED_LOG_BETA = -1e30
V7X_VMEM_BYTES = 64 * 1024 * 1024
VMEM_LIMIT = V7X_VMEM_BYTES * 3 // 4


def _const_spec(shape):
    zeros = (0,) * len(shape)
    return pl.BlockSpec(shape, lambda *_: zeros, pipeline_mode=pl.Buffered(1))


def _dot(a, b):
    return jnp.dot(a.astype(BF16), b.astype(BF16), preferred_element_type=F32)


def _dot_nt(a, b):
    return lax.dot_general(a.astype(BF16), b.astype(BF16), (((1,), (1,)), ((), ())),
                           preferred_element_type=F32)


def _split_bf16(x):
    hi = x.astype(BF16)
    return hi, (x - hi.astype(F32)).astype(BF16)


def _dot_split(a, b_hi, b_lo):
    a_hi, a_lo = _split_bf16(a)
    return (jnp.dot(a_hi, b_hi, preferred_element_type=F32)
            + jnp.dot(a_hi, b_lo, preferred_element_type=F32)
            + jnp.dot(a_lo, b_hi, preferred_element_type=F32))


def _prefix_sum(tri, x):
    hi = x.astype(BF16)
    rest = x - hi.astype(F32)
    mid = rest.astype(BF16)
    lo = (rest - mid.astype(F32)).astype(BF16)
    return (jnp.dot(tri, hi, preferred_element_type=F32)
            + jnp.dot(tri, mid, preferred_element_type=F32)
            + jnp.dot(tri, lo, preferred_element_type=F32))


def _rms_norm_rows(x, gain):
    ms = jnp.mean(x * x, axis=-1, keepdims=True)
    return x * lax.rsqrt(ms + RMS_EPS) * gain


def _head_reduce(t, bd, two_pass):
    outs = []
    for half in range(t.shape[1] // MXU_DIM):
        th = t[:, half * MXU_DIM:(half + 1) * MXU_DIM]
        hi = th.astype(BF16)
        red = jnp.dot(hi, bd, preferred_element_type=F32)
        if two_pass:
            lo = (th - hi.astype(F32)).astype(BF16)
            red = red + jnp.dot(lo, bd, preferred_element_type=F32)
        outs.append(red)
    return jnp.concatenate(outs, axis=1)


def _swiglu_residual(x, gain_ref, wg_ref, wu_ref, wd_ref, ff_tile):
    h = _rms_norm_rows(x, gain_ref[...]).astype(BF16)
    acc = x
    for start in range(0, wg_ref.shape[1], ff_tile):
        cols = slice(start, min(start + ff_tile, wg_ref.shape[1]))
        gate = jnp.dot(h, wg_ref[:, cols], preferred_element_type=F32)
        up = jnp.dot(h, wu_ref[:, cols], preferred_element_type=F32)
        act = (gate * jax.nn.sigmoid(gate) * up * 0.5).astype(BF16)
        acc = acc + jnp.dot(act, wd_ref[cols, :], preferred_element_type=F32)
    return acc


def _ffn_kernel(x_ref, gain_ref, wg_ref, wu_ref, wd_ref, o_ref, *, ff_tile):
    o_ref[...] = _swiglu_residual(x_ref[...], gain_ref, wg_ref, wu_ref, wd_ref, ff_tile)


def _proj_ffn_kernel(x_ref, osb_ref, orw_ref, wo_ref, gain_ref, wg_ref, wu_ref, wd_ref,
                     o_ref, *, ff_tile):
    mixed = (jnp.dot(osb_ref[...], wo_ref[:SB_WIDTH, :], preferred_element_type=F32)
             + jnp.dot(orw_ref[...], wo_ref[SB_WIDTH:, :], preferred_element_type=F32))
    x = mixed + x_ref[...]
    o_ref[...] = _swiglu_residual(x, gain_ref, wg_ref, wu_ref, wd_ref, ff_tile)


def _ffn_tiles(n_tokens, d_ff):
    row_tile = 512 if n_tokens % 512 == 0 else n_tokens
    ff_tile = -(-d_ff // (2 * MXU_DIM)) * MXU_DIM if d_ff % MXU_DIM == 0 else d_ff
    return row_tile, ff_tile


def _ffn_call(x, gain, wg, wu, wd, proj=None):
    n, d = x.shape
    d_ff = wg.shape[1]
    row_tile, ff_tile = _ffn_tiles(n, d_ff)
    row_spec = pl.BlockSpec((row_tile, d), lambda i: (i, 0))
    weight_specs = [_const_spec((1, d)), _const_spec((d, d_ff)), _const_spec((d, d_ff)),
                    _const_spec((d_ff, d))]
    if proj is None:
        body = functools.partial(_ffn_kernel, ff_tile=ff_tile)
        in_specs = [row_spec] + weight_specs
        args = (x, gain, wg, wu, wd)
        name = "ffn1"
    else:
        o_sb, o_rw, w_out = proj
        body = functools.partial(_proj_ffn_kernel, ff_tile=ff_tile)
        in_specs = [row_spec,
                    pl.BlockSpec((row_tile, SB_WIDTH), lambda i: (i, 0)),
                    pl.BlockSpec((row_tile, RW_WIDTH), lambda i: (i, 0)),
                    _const_spec(w_out.shape)] + weight_specs
        args = (x, o_sb, o_rw, w_out, gain, wg, wu, wd)
        name = "ffn2"
    return pl.pallas_call(
        body,
        out_shape=jax.ShapeDtypeStruct((n, d), F32),
        grid=(n // row_tile,),
        in_specs=in_specs,
        out_specs=row_spec,
        compiler_params=pltpu.CompilerParams(
            dimension_semantics=("parallel",), vmem_limit_bytes=VMEM_LIMIT),
        name=name,
    )(*args)


def _ffn_mix_kernel(x_ref, gain_ref, wg_ref, wu_ref, wd_ref,
                    mix_gain_ref, w_ref, qg_ref, kg_ref, mean_bd_ref,
                    o_ref, q_ref, k_ref, vt_ref, rw_ref, *, ff_tile):
    x1 = _swiglu_residual(x_ref[...], gain_ref, wg_ref, wu_ref, wd_ref, ff_tile)
    o_ref[...] = x1
    h = _rms_norm_rows(x1, mix_gain_ref[...]).astype(BF16)
    mean_bd = mean_bd_ref[...]

    def head_norm(t, gain):
        ms = _head_reduce(t * t, mean_bd, two_pass=False)
        return t * lax.rsqrt(ms + RMS_EPS) * gain

    q = jnp.dot(h, w_ref[:, 0:SB_WIDTH], preferred_element_type=F32)
    q_ref[...] = (head_norm(q, qg_ref[...]) * (HEAD_DIM ** -0.5)).astype(BF16)
    k = jnp.dot(h, w_ref[:, SB_WIDTH:2 * SB_WIDTH], preferred_element_type=F32)
    k_ref[...] = head_norm(k, kg_ref[...]).astype(BF16)
    vt = lax.dot_general(w_ref[:, 2 * SB_WIDTH:3 * SB_WIDTH], h, (((0,), (1,)), ((), ())),
                         preferred_element_type=F32).astype(BF16)
    for t in range(vt_ref.shape[0]):
        vt_ref[t] = vt[:, t * SB_TILE:(t + 1) * SB_TILE]
    rw_ref[...] = jnp.dot(h, w_ref[:, 3 * SB_WIDTH:], preferred_element_type=F32)


def _ffn_mix_call(x, gain, wg, wu, wd, mix_gain, w_in, q_gain, k_gain, mean_bd):
    n, d = x.shape
    d_ff = wg.shape[1]
    _, ff_tile = _ffn_tiles(n, d_ff)
    row_tile = 2 * SB_TILE
    assert n % row_tile == 0
    row = lambda width: pl.BlockSpec((row_tile, width), lambda i: (i, 0))
    return pl.pallas_call(
        functools.partial(_ffn_mix_kernel, ff_tile=ff_tile),
        out_shape=(jax.ShapeDtypeStruct((n, d), F32),
                   jax.ShapeDtypeStruct((n, SB_WIDTH), BF16),
                   jax.ShapeDtypeStruct((n, SB_WIDTH), BF16),
                   jax.ShapeDtypeStruct((n // SB_TILE, SB_WIDTH, SB_TILE), BF16),
                   jax.ShapeDtypeStruct((n, RW_PROJ), F32)),
        grid=(n // row_tile,),
        in_specs=[row(d), _const_spec((1, d)), _const_spec((d, d_ff)), _const_spec((d, d_ff)),
                  _const_spec((d_ff, d)), _const_spec((1, d)), _const_spec(w_in.shape),
                  _const_spec((1, SB_WIDTH)), _const_spec((1, SB_WIDTH)),
                  _const_spec(mean_bd.shape)],
        out_specs=(row(d), row(SB_WIDTH), row(SB_WIDTH),
                   pl.BlockSpec((row_tile // SB_TILE, SB_WIDTH, SB_TILE), lambda i: (i, 0, 0)),
                   row(RW_PROJ)),
        compiler_params=pltpu.CompilerParams(
            dimension_semantics=("parallel",), vmem_limit_bytes=VMEM_LIMIT),
        name="ffn1_mix_in",
    )(x, gain, wg, wu, wd, mix_gain, w_in, q_gain, k_gain, mean_bd)


def _sb_attn_kernel(q_ref, k_ref, vt_ref, gain_ref, o_ref, *scratch):
    def query_tile(tile, _):
        rows = pl.ds(pl.multiple_of(tile * SB_QTILE, SB_QTILE), SB_QTILE)
        _sb_attn_tile(tile, q_ref.at[rows, :], k_ref, vt_ref, gain_ref, o_ref.at[rows, :],
                      *scratch)
        return 0

    lax.fori_loop(0, q_ref.shape[0] // SB_QTILE, query_tile, 0)


def _sb_attn_tile(tile, q_ref, k_ref, vt_ref, gain_ref, o_ref, later_ref, acc_ref):
    n_heads = q_ref.shape[1] // HEAD_DIM
    first_head = lax.broadcasted_iota(jnp.int32, (LANES, SB_QTILE), 0) < HEAD_DIM
    q_pairs = []
    for p in range(n_heads // 2):
        q_t = q_ref[:, p * LANES:(p + 1) * LANES].astype(F32).T
        q_pairs.append(jnp.concatenate([jnp.where(first_head, q_t, 0.0),
                                        jnp.where(first_head, 0.0, q_t)], axis=1).astype(BF16))
    kk = lax.broadcasted_iota(jnp.int32, (SB_TILE, SB_TILE), 0)
    jj = lax.broadcasted_iota(jnp.int32, (SB_TILE, SB_TILE), 1)
    later_keys = jnp.where(jj > kk, 1.0, 0.0).astype(BF16)

    def key_tile(j, q_start, n_queries, masked):
        k_t = k_ref[pl.ds(pl.multiple_of(j * SB_TILE, SB_TILE), SB_TILE), :]
        v_t = vt_ref[j]
        q_cols = slice(q_start, q_start + n_queries)
        if masked:
            causal = (lax.broadcasted_iota(jnp.int32, (SB_TILE, n_queries), 0)
                      < lax.broadcasted_iota(jnp.int32, (SB_TILE, n_queries), 1))

        def logits(p):
            q_pair = jnp.concatenate([q_pairs[p][:, q_cols],
                                      q_pairs[p][:, SB_QTILE + q_start:SB_QTILE + q_start + n_queries]],
                                     axis=1)
            return jnp.dot(k_t[:, p * LANES:(p + 1) * LANES], q_pair, preferred_element_type=F32)

        def finish(h, log_beta, sums, key0_row):
            rows = slice(h * HEAD_DIM, (h + 1) * HEAD_DIM)
            cols = slice(h * SB_QTILE + q_start, h * SB_QTILE + q_start + n_queries)
            later = later_ref[:, cols]
            att = jnp.exp(log_beta + sums + later).astype(BF16)
            acc_ref[rows, q_cols] += jnp.dot(v_t[rows, :], att, preferred_element_type=F32)
            later_ref[:, cols] = later + sums[0:1, :] + key0_row

        z_pair = logits(0)
        pending = None
        for h in range(n_heads):
            if h % 2 == 0:
                z_now = z_pair
                if h + 2 < n_heads:
                    z_pair = logits(h // 2 + 1)
            z = z_now[:, (h % 2) * n_queries:(h % 2 + 1) * n_queries]
            neg_part = jnp.minimum(z, 0.0)
            neg_relu = neg_part - z
            soft = jnp.log(1.0 + jnp.exp(neg_part + neg_relu))
            log_beta = neg_part - soft
            log_1m_beta = neg_relu - soft
            if masked:
                log_beta = jnp.where(causal, log_beta, MASKED_LOG_BETA)
                log_1m_beta = jnp.where(causal, log_1m_beta, 0.0)
            sums = jnp.dot(later_keys, log_1m_beta.astype(BF16), preferred_element_type=F32)
            if pending is not None:
                finish(*pending)
            pending = (h, log_beta, sums, log_1m_beta[0:1, :])
        finish(*pending)

    later_ref[...] = jnp.zeros_like(later_ref)
    acc_ref[...] = jnp.zeros_like(acc_ref)
    ratio = SB_QTILE // SB_TILE
    for d in reversed(range(ratio)):
        key_tile(ratio * tile + d, d * SB_TILE, SB_QTILE - d * SB_TILE, masked=True)

    def earlier_tile(i, _):
        key_tile(ratio * tile - 1 - i, 0, SB_QTILE, masked=False)
        return 0

    lax.fori_loop(0, ratio * tile, earlier_tile, 0)

    def head_norm(acc):
        ms = jnp.sum(acc * acc, axis=0, keepdims=True) * (1.0 / HEAD_DIM)
        return acc * lax.rsqrt(ms + RMS_EPS)

    out_t = jnp.concatenate([head_norm(acc_ref[h * HEAD_DIM:(h + 1) * HEAD_DIM, :])
                             for h in range(n_heads)], axis=0)
    o_ref[...] = (out_t.T * gain_ref[...]).astype(o_ref.dtype)


def _sb_attn_call(q, k, v_t, out_gain):
    b, s, width = q.shape
    step_lanes = SB_HEADS_PER_STEP * HEAD_DIM
    assert s % SB_QTILE == 0 and width % step_lanes == 0
    seq_spec = pl.BlockSpec((None, s, step_lanes), lambda bi, p: (bi, 0, p))
    all_heads = SB_HEADS_PER_STEP * SB_QTILE
    return pl.pallas_call(
        _sb_attn_kernel,
        out_shape=jax.ShapeDtypeStruct((b, s, width), BF16),
        grid=(b, width // step_lanes),
        in_specs=[seq_spec, seq_spec,
                  pl.BlockSpec((s // SB_TILE, step_lanes, SB_TILE), lambda bi, p: (bi, p, 0)),
                  pl.BlockSpec((1, step_lanes), lambda bi, p: (0, p))],
        out_specs=seq_spec,
        scratch_shapes=[pltpu.VMEM((1, all_heads), F32),
                        pltpu.VMEM((step_lanes, SB_QTILE), F32)],
        compiler_params=pltpu.CompilerParams(
            dimension_semantics=("parallel", "parallel"), vmem_limit_bytes=VMEM_LIMIT),
        name="sb_attn",
    )(q, k, v_t, out_gain)


def _rwkv_kernel(u_ref, mu_ref, w0_ref, w2_ref, a0_ref, a2_ref, g2_ref, kk_ref, ka_ref,
                 rk_ref, lnw_ref, lnb_ref, ones_bd_ref, tri_ref, o_ref,
                 prev_ref, state_ref, r_s, kf_s, v_s, kn_s, eta_s, ld_s, y_s):
    n_rows = u_ref.shape[0]

    @pl.when(pl.program_id(1) == 0)
    def _():
        prev_ref[...] = jnp.zeros_like(prev_ref)
        state_ref[...] = jnp.zeros_like(state_ref)

    u = u_ref[...]
    row_id = lax.broadcasted_iota(jnp.int32, u.shape, 0)
    shifted = jnp.where(row_id == 0, prev_ref[...], pltpu.roll(u, 1, axis=0))
    prev_ref[...] = u[n_rows - 1:n_rows, :]
    u = u + (shifted - u) * mu_ref[...]

    x_r = u[:, 0:RW_WIDTH]
    x_k = u[:, RW_WIDTH:2 * RW_WIDTH]
    x_v = u[:, 2 * RW_WIDTH:3 * RW_WIDTH]
    x_wa = u[:, 3 * RW_WIDTH:3 * RW_WIDTH + DECAY_LORA + AAA_LORA]
    x_g = u[:, 3 * RW_WIDTH + DECAY_LORA + AAA_LORA:]
    ones_bd = ones_bd_ref[...]

    pre = -(w0_ref[...] + _dot_split(jnp.tanh(x_wa), w2_ref[0], w2_ref[1]))
    log_w = -(jnp.maximum(pre, 0.0) + jnp.log(1.0 + jnp.exp(-jnp.abs(pre)))) - 0.5
    ld_s[...] = -jnp.exp(log_w)
    eta = jax.nn.sigmoid(a0_ref[...] + _dot(x_wa, a2_ref[...]))
    gate = _dot(jax.nn.sigmoid(x_g), g2_ref[...])
    kk = x_k * kk_ref[...]
    kk_scale = jnp.minimum(lax.rsqrt(_head_reduce(kk * kk, ones_bd, two_pass=True)),
                           1.0 / KK_NORM_FLOOR)
    kf = x_k * (1.0 + (eta - 1.0) * ka_ref[...])
    bonus = _head_reduce(x_r * kf * rk_ref[...], ones_bd, two_pass=False) * x_v
    r_s[...] = x_r
    kf_s[...] = kf
    v_s[...] = x_v
    kn_s[...] = kk * kk_scale
    eta_s[...] = eta

    lane = lax.broadcasted_iota(jnp.int32, (RW_CHUNK, LANES), 1)
    first_head = lane < HEAD_DIM
    ri = lax.broadcasted_iota(jnp.int32, (LANES, LANES), 0)
    ci = lax.broadcasted_iota(jnp.int32, (LANES, LANES), 1)
    same_head = (ri < HEAD_DIM) == (ci < HEAD_DIM)
    strict_lower = same_head & (ci < ri)
    lower = same_head & (ci <= ri)
    eye = (ri == ci).astype(F32)
    tri = tri_ref[...]

    def stack_heads(t):
        return jnp.concatenate([jnp.where(first_head, t, 0.0), jnp.where(first_head, 0.0, t)],
                               axis=0)

    n_pairs = RW_WIDTH // LANES

    def chunk_operands(rows):
        ld = ld_s[rows, :]
        cum = _prefix_sum(tri, ld)
        total = cum[RW_CHUNK - 1:RW_CHUNK, :]
        e_incl = jnp.exp(cum)
        e_inv = jnp.exp(-cum)
        e_tail = jnp.exp(total - cum)
        w_total = jnp.exp(total)
        kn = kn_s[rows, :]
        kf_c = kf_s[rows, :]
        b = kn * eta_s[rows, :]
        a_dec = -kn * jnp.exp(cum - ld)
        r_dec = r_s[rows, :] * e_incl
        b_inv = b * e_inv
        k_inv = kf_c * e_inv
        b_tail = b * e_tail
        k_tail = kf_c * e_tail
        v_c = v_s[rows, :]
        units = []
        for p in range(n_pairs):
            cols = slice(p * LANES, (p + 1) * LANES)
            units.append(dict(
                a_sm=stack_heads(a_dec[:, cols]), r_sm=stack_heads(r_dec[:, cols]),
                v_sm=stack_heads(v_c[:, cols]), b_inv=b_inv[:, cols], k_inv=k_inv[:, cols],
                b_tail_t=stack_heads(b_tail[:, cols]).T, k_tail_t=stack_heads(k_tail[:, cols]).T,
                decay_diag=eye * w_total[:, cols]))
        return units

    def chunk_group(g, _):
        row_slices = [pl.ds(pl.multiple_of((g * RW_GROUP + c) * RW_CHUNK, RW_CHUNK), RW_CHUNK)
                      for c in range(RW_GROUP)]
        units = [u for rows in row_slices for u in chunk_operands(rows)]
        scores = [_dot_nt(jnp.concatenate([u["a_sm"], u["r_sm"]], axis=0),
                          jnp.concatenate([u["b_inv"], u["k_inv"]], axis=0))
                  for u in units]

        def block_diag(s, mask):
            swapped = pltpu.roll(s, RW_CHUNK, axis=1)
            x_bd = jnp.concatenate([s[:RW_CHUNK], swapped[RW_CHUNK:]], axis=0)
            y_bd = jnp.concatenate([swapped[:RW_CHUNK], s[RW_CHUNK:]], axis=0)
            return jnp.where(mask, x_bd, 0.0), jnp.where(mask, y_bd, 0.0)

        l_ab, l_ak = zip(*[block_diag(s[:LANES], strict_lower) for s in scores])
        m_rb, m_rk = zip(*[block_diag(s[LANES:], lower) for s in scores])
        lakv = [_dot(l, u["v_sm"]) for l, u in zip(l_ak, units)]
        inv = [eye + l for l in l_ab]
        power = [_dot(l, l) for l in l_ab]
        for _ in range(4):
            both = [_dot(pw, jnp.concatenate([pw, iv], axis=1)) for pw, iv in zip(power, inv)]
            inv = [iv + x[:, LANES:] for iv, x in zip(inv, both)]
            power = [x[:, :LANES] for x in both]
        inv = [iv + _dot(pw, iv) for iv, pw in zip(inv, power)]
        pu = [_dot(iv, jnp.concatenate([u["a_sm"], lv], axis=1))
              for iv, u, lv in zip(inv, units, lakv)]
        zeros = jnp.zeros((LANES, LANES), F32)
        tail = [_dot(jnp.concatenate([jnp.concatenate([mb, mk], axis=1),
                                      jnp.concatenate([u["b_tail_t"], u["k_tail_t"]], axis=1)],
                                     axis=0),
                     jnp.concatenate([x, jnp.concatenate([zeros, u["v_sm"]], axis=1)], axis=0))
                for mb, mk, u, x in zip(m_rb, m_rk, units, pu)]
        states = [state_ref[p] for p in range(n_pairs)]
        for c, rows in enumerate(row_slices):
            new_states = []
            for p in range(n_pairs):
                i = c * n_pairs + p
                q_sm = units[i]["r_sm"] + tail[i][:LANES, :LANES]
                y_sm = _dot(q_sm, states[p]) + tail[i][:LANES, LANES:]
                g_bd = tail[i][LANES:, :LANES] + units[i]["decay_diag"]
                new_states.append(_dot(g_bd, states[p]) + tail[i][LANES:, LANES:])
                y_s[rows, p * LANES:(p + 1) * LANES] = y_sm[:RW_CHUNK, :] + y_sm[RW_CHUNK:, :]
            states = new_states
        for p in range(n_pairs):
            state_ref[p] = states[p]
        return 0

    lax.fori_loop(0, n_rows // (RW_CHUNK * RW_GROUP), chunk_group, 0)

    y = y_s[...]
    centered = y - _head_reduce(y, ones_bd, two_pass=False) * (1.0 / HEAD_DIM)
    var = _head_reduce(centered * centered, ones_bd, two_pass=False) * (1.0 / HEAD_DIM)
    yn = centered * lax.rsqrt(var + LNX_EPS) * lnw_ref[...] + lnb_ref[...]
    o_ref[...] = ((yn + bonus) * gate).astype(o_ref.dtype)


def _rwkv_call(rw_in, mu, w0, w2_pad, a0, a2_pad, g2, k_k, k_a, r_k, ln_w, ln_b, ones_bd, tri):
    b, s, width = rw_in.shape
    row_tile = next(t for t in (512, 256, 128) if s % t == 0)
    vec = lambda n: _const_spec((1, n))
    scratch_rows = pltpu.VMEM((row_tile, RW_WIDTH), F32)
    return pl.pallas_call(
        _rwkv_kernel,
        out_shape=jax.ShapeDtypeStruct((b, s, RW_WIDTH), BF16),
        grid=(b, s // row_tile),
        in_specs=[pl.BlockSpec((None, row_tile, width), lambda bi, j: (bi, j, 0)),
                  vec(width), vec(RW_WIDTH), _const_spec(w2_pad.shape), vec(RW_WIDTH),
                  _const_spec(a2_pad.shape), _const_spec(g2.shape), vec(RW_WIDTH),
                  vec(RW_WIDTH), vec(RW_WIDTH), vec(RW_WIDTH), vec(RW_WIDTH),
                  _const_spec(ones_bd.shape), _const_spec(tri.shape)],
        out_specs=pl.BlockSpec((None, row_tile, RW_WIDTH), lambda bi, j: (bi, j, 0)),
        scratch_shapes=[pltpu.VMEM((1, width), F32),
                        pltpu.VMEM((RW_WIDTH // LANES, LANES, LANES), F32)]
                       + [scratch_rows] * 7,
        compiler_params=pltpu.CompilerParams(
            dimension_semantics=("parallel", "arbitrary"), vmem_limit_bytes=VMEM_LIMIT),
        name="rwkv7",
    )(rw_in, mu, w0, w2_pad, a0, a2_pad, g2, k_k, k_a, r_k, ln_w, ln_b, ones_bd, tri)


def _head_block_diag(value, dtype):
    idx = jnp.arange(MXU_DIM) // HEAD_DIM
    return jnp.where(idx[:, None] == idx[None, :], value, 0.0).astype(dtype)


def kernel(x, norm_ffn1, ffn1_gate, ffn1_up, ffn1_down, norm_mix, w_in, sb_q_norm, sb_k_norm,
           sb_out_norm, rw_mu, rw_w0, rw_w2, rw_a0, rw_a2, rw_g2, rw_k_k, rw_k_a, rw_r_k,
           rw_ln_w, rw_ln_b, w_out, norm_ffn2, ffn2_gate, ffn2_up, ffn2_down):
    b, s, d = x.shape
    n = b * s
    heads = SB_WIDTH // HEAD_DIM
    mean_bd = _head_block_diag(1.0 / HEAD_DIM, BF16)
    ones_bd = _head_block_diag(1.0, BF16)
    tri = jnp.tril(jnp.ones((RW_CHUNK, RW_CHUNK), BF16))
    row = lambda t: t.reshape(1, -1)

    h = x.reshape(n, d)
    for l in range(norm_ffn1.shape[0]):
        h, q, k, v_t, rw_in = _ffn_mix_call(
            h, row(norm_ffn1[l]), ffn1_gate[l].astype(BF16), ffn1_up[l].astype(BF16),
            ffn1_down[l].astype(BF16), row(norm_mix[l]), w_in[l].astype(BF16),
            row(jnp.tile(sb_q_norm[l], heads)), row(jnp.tile(sb_k_norm[l], heads)), mean_bd)

        o_sb = _sb_attn_call(q.reshape(b, s, SB_WIDTH), k.reshape(b, s, SB_WIDTH), v_t,
                             row(sb_out_norm[l]))

        zeros = jnp.zeros((DECAY_LORA, RW_WIDTH), F32)
        w2_pad = jnp.stack(_split_bf16(jnp.concatenate([rw_w2[l], zeros], axis=0)))
        a2_pad = jnp.concatenate([zeros, rw_a2[l]], axis=0).astype(BF16)
        o_rw = _rwkv_call(rw_in.reshape(b, s, RW_PROJ), row(rw_mu[l]), row(rw_w0[l]), w2_pad,
                          row(rw_a0[l]), a2_pad, rw_g2[l].astype(BF16), row(rw_k_k[l]),
                          row(rw_k_a[l]), row(rw_r_k[l]), row(rw_ln_w[l]), row(rw_ln_b[l]),
                          ones_bd, tri)

        h = _ffn_call(h, row(norm_ffn2[l]), ffn2_gate[l].astype(BF16), ffn2_up[l].astype(BF16),
                      ffn2_down[l].astype(BF16),
                      proj=(o_sb.reshape(n, SB_WIDTH), o_rw.reshape(n, RW_WIDTH),
                            w_out[l].astype(BF16)))
    return h.reshape(b, s, d)
```
